```python
import math
import jax, jax.numpy as jnp
from jax import lax
import numpy as np


D_MODEL = 1024
BATCH = 32
SEQ = 2048
DEPTH = 2
DEC_BATCH = 8
DEC_SEQ = 32
PAST_LEN = 1024

CHUNK = 64
Q_BLOCK = 128
DA_HEADS = 4
DA_HD = 64
DA_QK = 2 * DA_HD
DA_VD = 2 * DA_HD
DA_WIDTH = DA_HEADS * DA_VD
M_HEADS = 4
M_HD = 128
M_WIDTH = M_HEADS * M_HD
CONV_W = 4
D_FF = -(-8 * D_MODEL // (3 * 256)) * 256
IN_SIZES = (DA_HEADS * DA_QK, DA_HEADS * DA_QK, DA_WIDTH, 2 * M_WIDTH, M_WIDTH, 2 * M_HEADS, M_WIDTH)
IN_COLS = int(sum(IN_SIZES))
IN_SPLITS = tuple(int(v) for v in np.cumsum(IN_SIZES)[:-1])
ALPHA = (2 * DEPTH) ** 0.25
BETA = (8 * DEPTH) ** -0.25
LN_EPS = 1e-5
F32 = jnp.float32

kernel_name = 'diffattn_mlstm_streaming_encoder_step'


def layer_norm(x, g=None, b=None):
    xf = x.astype(F32)
    mu = xf.mean(-1, keepdims=True)
    var = jnp.mean(jnp.square(xf - mu), -1, keepdims=True)
    y = (xf - mu) * lax.rsqrt(var + LN_EPS)
    if g is not None:
        y = y * g.astype(F32) + b.astype(F32)
    return y.astype(x.dtype)


def head_norm(h, g, center):
    hf = h.astype(F32)
    if center:
        hf = hf - hf.mean(-1, keepdims=True)
    y = hf * lax.rsqrt(jnp.mean(hf * hf, -1, keepdims=True) + LN_EPS)
    return y.reshape(*h.shape[:-2], -1) * g.astype(F32)


def alibi_slopes(n):
    return jnp.asarray([2.0 ** (-8.0 * (i + 1) / n) for i in range(n)], F32)


def diff_attention(q, k, v, q_pos, k_pos, lam):
    s = jnp.einsum('bqhcd,bkhcd->bhcqk', q.astype(F32), k.astype(F32)) * (DA_HD ** -0.5)
    dist = jnp.abs(q_pos[:, None] - k_pos[None, :]).astype(F32)
    bias = -alibi_slopes(DA_HEADS)[:, None, None, None] * dist
    visible = (k_pos[None, :] // CHUNK) <= (q_pos[:, None] // CHUNK)
    s = jnp.where(visible, s + bias, -jnp.inf)
    p = jax.nn.softmax(s, axis=-1)
    w = p[:, :, 0] - lam * p[:, :, 1]
    return jnp.einsum('bhqk,bkhd->bqhd', w, v.astype(F32))


def diff_attention_prompt(q, k, v, lam):
    B, S = q.shape[0], q.shape[1]
    nblk = S // Q_BLOCK
    k_pos = jnp.arange(S)
    qb = q.reshape(B, nblk, Q_BLOCK, *q.shape[2:]).swapaxes(0, 1)

    def one_block(args):
        q_blk, i = args
        q_pos = i * Q_BLOCK + jnp.arange(Q_BLOCK)
        return diff_attention(q_blk, k, v, q_pos, k_pos, lam)

    out = lax.map(one_block, (qb, jnp.arange(nblk)))
    return out.swapaxes(0, 1).reshape(B, S, DA_HEADS, DA_VD)


def mlstm_chunk(state, inp):
    C, n, m = state
    q, k, v, ig, lf = inp
    L = q.shape[1]
    b = jnp.cumsum(lf, axis=1)
    Dm = b[:, :, None, :] - b[:, None, :, :] + ig[:, None, :, :]
    causal = jnp.tril(jnp.ones((L, L), dtype=bool))
    Dm = jnp.where(causal[None, :, :, None], Dm, -jnp.inf)
    inter = b + m[:, None, :]
    m_t = jnp.maximum(inter, Dm.max(axis=2))
    w_intra = jnp.exp(Dm - m_t[:, :, None, :])
    w_inter = jnp.exp(inter - m_t)
    qk = jnp.einsum('bthd,bshd->btsh', q, k) * w_intra
    num = jnp.einsum('btsh,bshd->bthd', qk, v) + w_inter[..., None] * jnp.einsum('bhvd,bthd->bthv', C, q)
    den = qk.sum(axis=2) + w_inter * jnp.einsum('bhd,bthd->bth', n, q)
    h = num / jnp.maximum(jnp.abs(den), jnp.exp(-m_t))[..., None]
    bL = b[:, -1]
    dec_s = bL[:, None, :] - b + ig
    m_new = jnp.maximum(bL + m, dec_s.max(axis=1))
    ws = jnp.exp(dec_s - m_new[:, None, :])
    wc = jnp.exp(bL + m - m_new)
    C_new = wc[..., None, None] * C + jnp.einsum('bsh,bshv,bshd->bhvd', ws, v, k)
    n_new = wc[..., None] * n + jnp.einsum('bsh,bshd->bhd', ws, k)
    return (C_new, n_new, m_new), h


def mlstm_prompt(q, k, v, ig, lf):
    B, S = q.shape[0], q.shape[1]
    nc = S // CHUNK

    def chunks(a):
        return a.reshape(B, nc, CHUNK, *a.shape[2:]).swapaxes(0, 1)

    init = (jnp.zeros((B, M_HEADS, M_HD, M_HD), F32), jnp.zeros((B, M_HEADS, M_HD), F32),
            jnp.zeros((B, M_HEADS), F32))
    state, h = lax.scan(mlstm_chunk, init, (chunks(q), chunks(k), chunks(v), chunks(ig), chunks(lf)))
    return h.swapaxes(0, 1).reshape(B, S, M_HEADS, M_HD), state


def causal_conv(u, buf, w, b):
    T = u.shape[1]
    full = jnp.concatenate([buf.astype(u.dtype), u], axis=1)
    y = b
    for j in range(CONV_W):
        y = y + full[:, j:j + T] * w[j]
    return jax.nn.silu(y), full[:, -(CONV_W - 1):]


def trunk_layer(x, c, layer, attend, recur, conv_buf,
                w_ada, b_ada, w_in, b_if, conv_w, conv_b, lam_p, da_norm_w, m_norm_w,
                w_br_a, w_br_b, w_gate, b_gate, w_o, ln1_g, ln1_b, w_gu, w_down, ln2_g, ln2_b):
    B, T, _ = x.shape
    mod = jnp.einsum('bd,de->be', jax.nn.silu(c), w_ada) + b_ada
    sh1, sc1, g1, sh2, sc2, g2 = jnp.split(mod[:, None, :], 6, axis=-1)
    h = layer_norm(x) * (1 + sc1) + sh1
    z = jnp.einsum('btd,de->bte', h, w_in)
    a_q, a_k, a_v, m_qk, m_v, m_if, m_o = jnp.split(z, IN_SPLITS, axis=-1)
    aq = a_q.reshape(B, T, DA_HEADS, 2, DA_HD)
    ak = a_k.reshape(B, T, DA_HEADS, 2, DA_HD)
    av = a_v.reshape(B, T, DA_HEADS, DA_VD)
    lam_init = 0.8 - 0.6 * math.exp(-0.3 * layer)
    lp = lam_p.astype(F32)
    lam = jnp.exp(jnp.sum(lp[0] * lp[1])) - jnp.exp(jnp.sum(lp[2] * lp[3])) + lam_init
    a_out = attend(aq, ak, av, lam)
    qk_c, conv_state = causal_conv(m_qk, conv_buf, conv_w, conv_b)
    mq, mk = jnp.split(qk_c, 2, axis=-1)
    mq = mq.reshape(B, T, M_HEADS, M_HD).astype(F32)
    mk = mk.reshape(B, T, M_HEADS, M_HD).astype(F32) * (M_HD ** -0.5)
    mv = m_v.reshape(B, T, M_HEADS, M_HD).astype(F32)
    gates = (m_if + b_if).astype(F32)
    ig = gates[..., :M_HEADS]
    lf = jax.nn.log_sigmoid(gates[..., M_HEADS:])
    m_out, m_state = recur(mq, mk, mv, ig, lf)
    a_n = head_norm(a_out, da_norm_w, False) * (1.0 - lam_init)
    m_n = head_norm(m_out, m_norm_w, True) * jax.nn.sigmoid(m_o.astype(F32))
    y_a = a_n.astype(x.dtype) @ w_br_a
    y_b = m_n.astype(x.dtype) @ w_br_b
    g_a, g_b = jnp.split(jax.nn.sigmoid(h @ w_gate + b_gate), 2, axis=-1)
    mix = (g_a * y_a + g_b * y_b) @ w_o
    x = layer_norm(ALPHA * x + (1 + g1) * mix, ln1_g, ln1_b)
    h2 = layer_norm(x) * (1 + sc2) + sh2
    gt, up = jnp.split(h2 @ w_gu, 2, axis=-1)
    ffn = (jax.nn.silu(gt) * up) @ w_down
    x = layer_norm(ALPHA * x + (1 + g2) * ffn, ln2_g, ln2_b)
    k_rows = ak.reshape(B, T, DA_HEADS, DA_QK)
    return x.astype(c.dtype), k_rows, av, m_state, conv_state


def setup_inputs(seed: int = 0) -> dict:
    key = jax.random.key(seed)
    ks = iter(jax.random.split(key, 48))

    def nrm(shape, s):
        return jax.random.normal(next(ks), shape, F32) * s

    col_scale = np.ones((IN_COLS,), np.float32)
    off = np.cumsum((0,) + IN_SIZES)
    col_scale[off[2]:off[3]] = BETA
    col_scale[off[4]:off[5]] = BETA
    b_if = jnp.concatenate([nrm((DEPTH, M_HEADS), 0.1),
                            jnp.broadcast_to(jnp.linspace(3.0, 6.0, M_HEADS), (DEPTH, M_HEADS)) + nrm((DEPTH, M_HEADS), 0.1)], -1)
    return {
        'x_prompt': nrm((BATCH, SEQ, D_MODEL), 1.0),
        'x_sample': nrm((DEC_BATCH, DEC_SEQ, D_MODEL), 1.0),
        'c_prompt': nrm((BATCH, D_MODEL), 1.0),
        'c_sample': nrm((DEC_BATCH, D_MODEL), 1.0),
        'cache_attn_k': nrm((DEPTH, DEC_BATCH, PAST_LEN, DA_HEADS, DA_QK), 1.0),
        'cache_attn_v': nrm((DEPTH, DEC_BATCH, PAST_LEN, DA_HEADS, DA_VD), 0.5),
        'state_mlstm_C': nrm((DEPTH, DEC_BATCH, M_HEADS, M_HD, M_HD), 0.3),
        'state_mlstm_n': nrm((DEPTH, DEC_BATCH, M_HEADS, M_HD), 0.3),
        'state_mlstm_m': nrm((DEPTH, DEC_BATCH, M_HEADS), 1.0),
        'state_mlstm_conv': nrm((DEPTH, DEC_BATCH, CONV_W - 1, 2 * M_WIDTH), 1.0),
        'w_ada': nrm((DEPTH, D_MODEL, 6 * D_MODEL), 0.2 * D_MODEL ** -0.5),
        'b_ada': nrm((DEPTH, 6 * D_MODEL), 0.01),
        'w_in': nrm((DEPTH, D_MODEL, IN_COLS), D_MODEL ** -0.5) * jnp.asarray(col_scale),
        'b_if': b_if,
        'conv_w': nrm((DEPTH, CONV_W, 2 * M_WIDTH), CONV_W ** -0.5),
        'conv_b': nrm((DEPTH, 2 * M_WIDTH), 0.01),
        'lam_p': nrm((DEPTH, 4, DA_HD), 0.1),
        'da_norm_w': 1.0 + nrm((DEPTH, DA_WIDTH), 0.02),
        'm_norm_w': 1.0 + nrm((DEPTH, M_WIDTH), 0.02),
        'w_br_a': nrm((DEPTH, DA_WIDTH, D_MODEL), BETA * DA_WIDTH ** -0.5),
        'w_br_b': nrm((DEPTH, M_WIDTH, D_MODEL), BETA * M_WIDTH ** -0.5),
        'w_gate': nrm((DEPTH, D_MODEL, 2 * D_MODEL), D_MODEL ** -0.5),
        'b_gate': nrm((DEPTH, 2 * D_MODEL), 0.01),
        'w_o': nrm((DEPTH, D_MODEL, D_MODEL), BETA * D_MODEL ** -0.5),
        'ln1_g': 1.0 + nrm((DEPTH, D_MODEL), 0.02),
        'ln1_b': nrm((DEPTH, D_MODEL), 0.01),
        'w_gu': nrm((DEPTH, D_MODEL, 2 * D_FF), D_MODEL ** -0.5),
        'w_down': nrm((DEPTH, D_FF, D_MODEL), BETA * D_FF ** -0.5),
        'ln2_g': 1.0 + nrm((DEPTH, D_MODEL), 0.02),
        'ln2_b': nrm((DEPTH, D_MODEL), 0.01),
    }


def reference(x_prompt, x_sample, c_prompt, c_sample, cache_attn_k, cache_attn_v,
              state_mlstm_C, state_mlstm_n, state_mlstm_m, state_mlstm_conv,
              w_ada, b_ada, w_in, b_if, conv_w, conv_b, lam_p, da_norm_w, m_norm_w,
              w_br_a, w_br_b, w_gate, b_gate, w_o, ln1_g, ln1_b, w_gu, w_down, ln2_g, ln2_b):
    xp, xs = x_prompt, x_sample
    Bp = xp.shape[0]
    Bs, Ts = xs.shape[0], xs.shape[1]
    P = cache_attn_k.shape[2]
    kp_l, vp_l, Cp_l, np_l, mp_l, cvp_l = [], [], [], [], [], []
    ks_l, vs_l, Cs_l, ns_l, ms_l, cvs_l = [], [], [], [], [], []
    for l in range(DEPTH):
        weights = (w_ada[l], b_ada[l], w_in[l], b_if[l], conv_w[l], conv_b[l], lam_p[l], da_norm_w[l],
                   m_norm_w[l], w_br_a[l], w_br_b[l], w_gate[l], b_gate[l], w_o[l], ln1_g[l], ln1_b[l],
                   w_gu[l], w_down[l], ln2_g[l], ln2_b[l])
        conv0 = jnp.zeros((Bp, CONV_W - 1, 2 * M_WIDTH), xp.dtype)
        xp, k_new, v_new, (C_f, n_f, m_f), cv = trunk_layer(
            xp, c_prompt, l, diff_attention_prompt, mlstm_prompt, conv0, *weights)
        kp_l.append(k_new); vp_l.append(v_new); Cp_l.append(C_f); np_l.append(n_f); mp_l.append(m_f); cvp_l.append(cv)

        ck = cache_attn_k[l].reshape(Bs, P, DA_HEADS, 2, DA_HD)
        cvv = cache_attn_v[l]

        def attend_sample(q, k, v, lam, ck=ck, cvv=cvv):
            k_all = jnp.concatenate([ck.astype(k.dtype), k], axis=1)
            v_all = jnp.concatenate([cvv.astype(v.dtype), v], axis=1)
            q_pos = P + jnp.arange(Ts)
            k_pos = jnp.arange(P + Ts)
            return diff_attention(q, k_all, v_all, q_pos, k_pos, lam)

        st = (state_mlstm_C[l].astype(F32), state_mlstm_n[l].astype(F32), state_mlstm_m[l].astype(F32))

        def recur_sample(q, k, v, ig, lf, st=st):
            st_new, h = mlstm_chunk(st, (q, k, v, ig, lf))
            return h, st_new

        xs, k_new, v_new, (C_f, n_f, m_f), cv = trunk_layer(
            xs, c_sample, l, attend_sample, recur_sample, state_mlstm_conv[l], *weights)
        ks_l.append(k_new); vs_l.append(v_new); Cs_l.append(C_f); ns_l.append(n_f); ms_l.append(m_f); cvs_l.append(cv)

    attn_k_prompt = jnp.stack(kp_l); attn_v_prompt = jnp.stack(vp_l)
    attn_k_sample = jnp.stack(ks_l); attn_v_sample = jnp.stack(vs_l)
    mlstm_C_prompt = jnp.stack(Cp_l); mlstm_n_prompt = jnp.stack(np_l)
    mlstm_m_prompt = jnp.stack(mp_l); mlstm_conv_prompt = jnp.stack(cvp_l)
    mlstm_C_sample = jnp.stack(Cs_l); mlstm_n_sample = jnp.stack(ns_l)
    mlstm_m_sample = jnp.stack(ms_l); mlstm_conv_sample = jnp.stack(cvs_l)
    return (xp, xs, attn_k_prompt, attn_v_prompt, attn_k_sample, attn_v_sample,
            mlstm_C_prompt, mlstm_n_prompt, mlstm_m_prompt, mlstm_conv_prompt,
            mlstm_C_sample, mlstm_n_sample, mlstm_m_sample, mlstm_conv_sample)
```

```python
import functools
import math

import jax
import jax.numpy as jnp
from jax import lax
from jax.experimental import pallas as pl
from jax.experimental.pallas import tpu as pltpu

F32 = jnp.float32
BF16 = jnp.bfloat16

D_MODEL = 1024
CHUNK = 64
DA_HEADS = 4
DA_HD = 64
DA_QK = 2 * DA_HD
DA_VD = 2 * DA_HD
DA_WIDTH = DA_HEADS * DA_VD
M_HEADS = 4
M_HD = 128
M_WIDTH = M_HEADS * M_HD
CONV_W = 4
N_GATES = 2 * M_HEADS
LN_EPS = 1e-5

SUBLANES = 8
LANES = 128
VMEM_LIMIT_BYTES = 56 * 1024 * 1024

_NT = (((1,), (1,)), ((), ()))
_TN = (((0,), (0,)), ((), ()))


def _dot(a, b):
    return jnp.dot(a, b, preferred_element_type=F32)


def _dot_nt(a, b):
    return lax.dot_general(a, b, _NT, preferred_element_type=F32)


def _dot_tn(a, b):
    return lax.dot_general(a, b, _TN, preferred_element_type=F32)


def _layer_norm(x):
    mu = jnp.mean(x, axis=-1, keepdims=True)
    xc = x - mu
    var = jnp.mean(xc * xc, axis=-1, keepdims=True)
    return xc * lax.rsqrt(var + LN_EPS)


def _params(*sem):
    return pltpu.CompilerParams(dimension_semantics=sem, vmem_limit_bytes=VMEM_LIMIT_BYTES)


def _const_spec(shape):
    nd = len(shape)
    return pl.BlockSpec(shape, lambda *_: (0,) * nd, pipeline_mode=pl.Buffered(1))


def _ada_kernel(c_ref, w_ref, b_ref, o_ref):
    s = jax.nn.silu(c_ref[...]).astype(BF16)
    o_ref[0] = _dot(s, w_ref[0]) + b_ref[0]


def _ada(c_all, w_ada, b_ada):
    depth, d, n = w_ada.shape
    r = c_all.shape[0]
    tn = 2048
    return pl.pallas_call(
        _ada_kernel,
        grid=(depth, n // tn),
        in_specs=[
            pl.BlockSpec((r, d), lambda l, j: (0, 0)),
            pl.BlockSpec((1, d, tn), lambda l, j: (l, 0, j)),
            pl.BlockSpec((1, 1, tn), lambda l, j: (l, 0, j)),
        ],
        out_specs=pl.BlockSpec((1, r, tn), lambda l, j: (l, 0, j)),
        out_shape=jax.ShapeDtypeStruct((depth, r, n), F32),
        compiler_params=_params("parallel", "parallel"),
        name="ada_mod",
    )(c_all, w_ada, b_ada.reshape(depth, 1, n))


def _inproj_kernel(x_ref, sc_ref, sh_ref, w_ref, wif_ref, wift_ref,
                   q_ref, k_ref, v_ref, mqk_ref, mv_ref, mo_ref, gc_ref, gr_ref, *, n_chunks, chunk):
    x = x_ref[0]
    h = (_layer_norm(x) * (1.0 + sc_ref[0]) + sh_ref[0]).astype(BF16)

    def proj(lo, hi):
        return _dot(h, w_ref[:, lo:hi])

    q_ref[0] = (proj(0, 512) * (DA_HD ** -0.5)).astype(BF16)
    k_ref[0] = proj(512, 1024)
    v_ref[0] = proj(1024, 1536)
    mqk_ref[0] = proj(1536, 2560)
    mv_ref[0] = proj(2560, 3072).astype(BF16)
    mo_ref[0] = proj(3072, 3584)
    gc_ref[0] = _dot(h, wif_ref[...])
    gr = _dot_nt(wift_ref[...], h)
    for j in range(n_chunks):
        gr_ref[0, j] = gr[:, j * chunk:(j + 1) * chunk]


def _inproj(x, sc, sh, w_a, w_if, w_ift, *, tm, chunk):
    b, t, d = x.shape
    n_chunks = tm // chunk
    tok = lambda n: pl.BlockSpec((1, tm, n), lambda i, j: (i, j, 0))
    row = pl.BlockSpec((1, 1, d), lambda i, j: (i, 0, 0))
    out_shape = (
        jax.ShapeDtypeStruct((b, t, DA_WIDTH), BF16),
        jax.ShapeDtypeStruct((b, t, DA_WIDTH), F32),
        jax.ShapeDtypeStruct((b, t, DA_WIDTH), F32),
        jax.ShapeDtypeStruct((b, t, 2 * M_WIDTH), F32),
        jax.ShapeDtypeStruct((b, t, M_WIDTH), BF16),
        jax.ShapeDtypeStruct((b, t, M_WIDTH), F32),
        jax.ShapeDtypeStruct((b, t, LANES), F32),
        jax.ShapeDtypeStruct((b, t // chunk, N_GATES, chunk), F32),
    )
    out_specs = (
        tok(DA_WIDTH), tok(DA_WIDTH), tok(DA_WIDTH), tok(2 * M_WIDTH), tok(M_WIDTH), tok(M_WIDTH), tok(LANES),
        pl.BlockSpec((1, n_chunks, N_GATES, chunk), lambda i, j: (i, j, 0, 0)),
    )
    return pl.pallas_call(
        functools.partial(_inproj_kernel, n_chunks=n_chunks, chunk=chunk),
        grid=(b, t // tm),
        in_specs=[tok(d), row, row, _const_spec(w_a.shape), _const_spec(w_if.shape), _const_spec(w_ift.shape)],
        out_specs=out_specs,
        out_shape=out_shape,
        compiler_params=_params("parallel", "parallel"),
        name="in_proj",
    )(x, sc, sh, w_a, w_if, w_ift)


def _lambda(lp_ref, lam_init):
    lp = lp_ref[...]
    a = jnp.sum(lp[0:1] * lp[1:2], axis=1, keepdims=True)
    b = jnp.sum(lp[2:3] * lp[3:4], axis=1, keepdims=True)
    return jnp.exp(a) - jnp.exp(b) + lam_init


def _alibi_slope(h):
    return jnp.where(h == 0, 2.0 ** -2, jnp.where(h == 1, 2.0 ** -4, jnp.where(h == 2, 2.0 ** -6, 2.0 ** -8))).astype(F32)


def _attn_prompt_kernel(q_ref, k_ref, v_ref, lp_ref, g_ref, o_ref,
                        kb, vt, acc, mst, lst, *, tq, t, lam_init):
    h = pl.program_id(1)
    qi = pl.program_id(2)
    tk = tq

    @pl.when(qi == 0)
    def _():
        for c in range(t // tk):
            rows = slice(c * tk, (c + 1) * tk)
            kb[rows, :] = k_ref[0, rows, :].astype(BF16)
            vt[:, rows] = v_ref[0, rows, :].T.astype(BF16)

    q = q_ref[0]
    lane = lax.broadcasted_iota(jnp.int32, q.shape, 1)
    zero = jnp.zeros_like(q)
    qc = (jnp.where(lane < DA_HD, q, zero), jnp.where(lane >= DA_HD, q, zero))

    slope = _alibi_slope(h)
    krow = lax.broadcasted_iota(jnp.int32, (tk, tq), 0)
    qcol = lax.broadcasted_iota(jnp.int32, (tk, tq), 1)
    rel = (qcol - krow).astype(F32)
    visible_diag = (krow // CHUNK) <= (qcol // CHUNK)

    mst[...] = jnp.full(mst.shape, -jnp.inf, F32)
    lst[...] = jnp.zeros(lst.shape, F32)
    acc[...] = jnp.zeros(acc.shape, F32)

    def block(kj, masked):
        off = pl.multiple_of(kj * tk, tk)
        kblk = kb[pl.ds(off, tk), :]
        vblk = vt[:, pl.ds(off, tk)]
        dist = jnp.abs(rel + ((qi - kj) * tq).astype(F32))
        bias = -slope * dist
        for c in range(2):
            s = _dot_nt(kblk, qc[c]) + bias
            if masked:
                s = jnp.where(visible_diag, s, -jnp.inf)
            m_old = mst[c:c + 1, :]
            m_new = jnp.maximum(m_old, jnp.max(s, axis=0, keepdims=True))
            alpha = jnp.exp(m_old - m_new)
            p = jnp.exp(s - m_new)
            lst[c:c + 1, :] = alpha * lst[c:c + 1, :] + jnp.sum(p, axis=0, keepdims=True)
            acc[c] = alpha * acc[c] + _dot(vblk, p.astype(BF16))
            mst[c:c + 1, :] = m_new

    def body(kj, carry):
        block(kj, masked=False)
        return carry

    lax.fori_loop(0, qi, body, 0)
    block(qi, masked=True)

    lam = _lambda(lp_ref, lam_init)
    o = acc[0] / lst[0:1, :] - lam * (acc[1] / lst[1:2, :])
    o = o * lax.rsqrt(jnp.mean(o * o, axis=0, keepdims=True) + LN_EPS)
    o_ref[0] = (o.T * g_ref[...] * (1.0 - lam_init)).astype(BF16)


def _attn_prompt(q, k, v, lam_p, gain, *, lam_init, tq):
    b, t, _ = q.shape
    kv = pl.BlockSpec((1, t, DA_QK), lambda i, h, j: (i, 0, h))
    return pl.pallas_call(
        functools.partial(_attn_prompt_kernel, tq=tq, t=t, lam_init=lam_init),
        grid=(b, DA_HEADS, t // tq),
        in_specs=[
            pl.BlockSpec((1, tq, DA_QK), lambda i, h, j: (i, j, h)),
            kv, kv,
            pl.BlockSpec(lam_p.shape, lambda i, h, j: (0, 0)),
            pl.BlockSpec((1, DA_VD), lambda i, h, j: (0, h)),
        ],
        out_specs=pl.BlockSpec((1, tq, DA_VD), lambda i, h, j: (i, j, h)),
        out_shape=jax.ShapeDtypeStruct((b, t, DA_WIDTH), BF16),
        scratch_shapes=[
            pltpu.VMEM((t, DA_QK), BF16),
            pltpu.VMEM((DA_VD, t), BF16),
            pltpu.VMEM((2, DA_VD, tq), F32),
            pltpu.VMEM((SUBLANES, tq), F32),
            pltpu.VMEM((SUBLANES, tq), F32),
        ],
        compiler_params=_params("parallel", "parallel", "arbitrary"),
        name="diff_attn_prompt",
    )(q, k, v, lam_p, gain)


def _attn_sample_kernel(q_ref, k_ref, v_ref, ck_ref, cv_ref, lp_ref, g_ref, o_ref, *, past, ts, lam_init):
    h = pl.program_id(1)
    q = q_ref[0]
    lane = lax.broadcasted_iota(jnp.int32, q.shape, 1)
    zero = jnp.zeros_like(q)
    qc = (jnp.where(lane < DA_HD, q, zero), jnp.where(lane >= DA_HD, q, zero))
    slope = _alibi_slope(h)

    def bias_mask(n_keys, key_base):
        qpos = past + lax.broadcasted_iota(jnp.int32, (ts, n_keys), 0)
        kpos = key_base + lax.broadcasted_iota(jnp.int32, (ts, n_keys), 1)
        bias = -slope * jnp.abs(qpos - kpos).astype(F32)
        return bias, (kpos // CHUNK) <= (qpos // CHUNK)

    ck = ck_ref[0].astype(BF16)
    cv = cv_ref[0].astype(BF16)
    kn = k_ref[0].astype(BF16)
    vn = v_ref[0].astype(BF16)
    bias_c, vis_c = bias_mask(past, 0)
    bias_n, vis_n = bias_mask(ts, past)
    outs = []
    for c in range(2):
        s_c = jnp.where(vis_c, _dot_nt(qc[c], ck) + bias_c, -jnp.inf)
        s_n = jnp.where(vis_n, _dot_nt(qc[c], kn) + bias_n, -jnp.inf)
        m = jnp.maximum(jnp.max(s_c, axis=1, keepdims=True), jnp.max(s_n, axis=1, keepdims=True))
        p_c = jnp.exp(s_c - m)
        p_n = jnp.exp(s_n - m)
        l = jnp.sum(p_c, axis=1, keepdims=True) + jnp.sum(p_n, axis=1, keepdims=True)
        outs.append((_dot(p_c.astype(BF16), cv) + _dot(p_n.astype(BF16), vn)) / l)
    lam = _lambda(lp_ref, lam_init)
    o = outs[0] - lam * outs[1]
    o = o * lax.rsqrt(jnp.mean(o * o, axis=1, keepdims=True) + LN_EPS)
    o_ref[0] = (o * g_ref[...] * (1.0 - lam_init)).astype(BF16)


def _attn_sample(q, k, v, cache_k, cache_v, lam_p, gain, *, lam_init):
    b, ts, _ = q.shape
    past = cache_k.shape[1]
    new = pl.BlockSpec((1, ts, DA_QK), lambda i, h: (i, 0, h))
    old = pl.BlockSpec((1, past, DA_QK), lambda i, h: (i, 0, h))
    return pl.pallas_call(
        functools.partial(_attn_sample_kernel, past=past, ts=ts, lam_init=lam_init),
        grid=(b, DA_HEADS),
        in_specs=[new, new, new, old, old,
                  pl.BlockSpec(lam_p.shape, lambda i, h: (0, 0)),
                  pl.BlockSpec((1, DA_VD), lambda i, h: (0, h))],
        out_specs=new,
        out_shape=jax.ShapeDtypeStruct((b, ts, DA_WIDTH), BF16),
        compiler_params=_params("parallel", "parallel"),
        name="diff_attn_sample",
    )(q, k, v, cache_k, cache_v, lam_p, gain)


def _split3(x):
    hi = x.astype(BF16)
    r = x - hi.astype(F32)
    mid = r.astype(BF16)
    lo = (r - mid.astype(F32)).astype(BF16)
    return hi, mid, lo


def _mlstm_kernel(mqk_ref, mv_ref, mo_ref, gc_ref, gr_ref, cw_ref, cb_ref, bc_ref, br_ref, nw_ref,
                  c0_ref, n0_ref, m0_ref, cv0_ref,
                  o_ref, cf_ref, nf_ref, mf_ref, cvf_ref,
                  ubuf, qk_s, c_s, n_s, m_s, *, chunk, tb):
    t = pl.program_id(1)
    n_t = pl.num_programs(1)
    width = 2 * M_WIDTH

    @pl.when(t == 0)
    def _():
        ubuf[0:SUBLANES, :] = cv0_ref[0]
        c_s[...] = c0_ref[0]
        n_s[...] = n0_ref[0]
        m_s[...] = m0_ref[0]

    ubuf[SUBLANES:SUBLANES + tb, :] = mqk_ref[0]
    k_scale = M_HD ** -0.5
    for j in range(width // LANES):
        cols = slice(j * LANES, (j + 1) * LANES)
        y = cb_ref[:, cols]
        for tap in range(CONV_W):
            start = SUBLANES - (CONV_W - 1) + tap
            y = y + ubuf[start:start + tb, cols] * cw_ref[tap:tap + 1, cols]
        y = jax.nn.silu(y)
        if j >= M_WIDTH // LANES:
            y = y * k_scale
        qk_s[:, cols] = y.astype(BF16)
    ubuf[0:SUBLANES, :] = ubuf[tb:tb + SUBLANES, :]

    row_i = lax.broadcasted_iota(jnp.int32, (chunk, chunk), 0)
    col_i = lax.broadcasted_iota(jnp.int32, (chunk, chunk), 1)
    causal = col_i <= row_i
    tril = causal.astype(BF16)
    triu = (row_i <= col_i).astype(BF16)

    def chunk_body(ci, carry):
        off = pl.multiple_of(ci * chunk, chunk)
        rows = pl.ds(off, chunk)
        gcb = gc_ref[0, rows, :] + bc_ref[...]
        lf_c = jax.nn.log_sigmoid(gcb)
        grb = gr_ref[0, ci] + br_ref[...]
        lf_r = jax.nn.log_sigmoid(grb)
        b_cols = sum(_dot(tril, part) for part in _split3(lf_c))
        b_rows = sum(_dot(part, triu) for part in _split3(lf_r))

        for h in range(M_HEADS):
            hc = slice(h * M_HD, (h + 1) * M_HD)
            q = qk_s[rows, hc]
            k = qk_s[rows, M_WIDTH + h * M_HD:M_WIDTH + (h + 1) * M_HD]
            v = mv_ref[0, rows, hc]
            b_col = b_cols[:, M_HEADS + h:M_HEADS + h + 1]
            ig_col = gcb[:, h:h + 1]
            b_row = b_rows[M_HEADS + h:M_HEADS + h + 1, :]
            ig_row = grb[h:h + 1, :]
            m_prev = m_s[h:h + 1, 0:1]

            dm = jnp.where(causal, b_col - b_row + ig_row, -jnp.inf)
            inter = b_col + m_prev
            m_t = jnp.maximum(inter, jnp.max(dm, axis=1, keepdims=True))
            w_intra = jnp.exp(dm - m_t)
            w_inter = jnp.exp(inter - m_t)
            p = _dot_nt(q, k) * w_intra
            c_prev = c_s[h]
            n_prev = n_s[h:h + 1, :]
            num = _dot(p.astype(BF16), v) + w_inter * _dot_nt(q, c_prev.astype(BF16))
            qn = jnp.sum(q.astype(F32) * n_prev, axis=1, keepdims=True)
            den = jnp.sum(p, axis=1, keepdims=True) + w_inter * qn
            hout = num / jnp.maximum(jnp.abs(den), jnp.exp(-m_t))

            b_last = b_col[chunk - 1:chunk, :]
            dec = b_last - b_col + ig_col
            m_new = jnp.maximum(b_last + m_prev, jnp.max(dec, axis=0, keepdims=True))
            ws = jnp.exp(dec - m_new)
            wc = jnp.exp(b_last + m_prev - m_new)
            c_s[h] = wc * c_prev + _dot_tn((v.astype(F32) * ws).astype(BF16), k)
            n_s[h:h + 1, :] = wc * n_prev + jnp.sum(ws * k.astype(F32), axis=0, keepdims=True)
            m_s[h:h + 1, :] = jnp.broadcast_to(m_new, (1, LANES))

            hcen = hout - jnp.mean(hout, axis=1, keepdims=True)
            hn = hcen * lax.rsqrt(jnp.mean(hcen * hcen, axis=1, keepdims=True) + LN_EPS)
            o_ref[0, rows, hc] = (hn * nw_ref[:, hc] * jax.nn.sigmoid(mo_ref[0, rows, hc])).astype(BF16)
        return carry

    lax.fori_loop(0, tb // chunk, chunk_body, 0)

    @pl.when(t == n_t - 1)
    def _():
        cf_ref[0] = c_s[...]
        nf_ref[0] = n_s[...]
        mf_ref[0] = m_s[...]
        cvf_ref[0] = ubuf[0:SUBLANES, :]


def _mlstm(mqk, mv, mo, gc, gr, conv_w, conv_b, b_if, norm_w, c0, n0, m0, cv0, *, chunk, tb):
    b, t, width = mqk.shape
    tok = lambda n: pl.BlockSpec((1, tb, n), lambda i, j: (i, j, 0))
    per_b = lambda shape: pl.BlockSpec((1,) + shape, lambda i, j: (i,) + (0,) * len(shape))
    bc = jnp.zeros((1, LANES), F32).at[0, :N_GATES].set(b_if)
    br = b_if.reshape(N_GATES, 1)
    out_shape = (
        jax.ShapeDtypeStruct((b, t, M_WIDTH), BF16),
        jax.ShapeDtypeStruct((b, M_HEADS, M_HD, M_HD), F32),
        jax.ShapeDtypeStruct((b, SUBLANES, M_HD), F32),
        jax.ShapeDtypeStruct((b, SUBLANES, LANES), F32),
        jax.ShapeDtypeStruct((b, SUBLANES, width), F32),
    )
    return pl.pallas_call(
        functools.partial(_mlstm_kernel, chunk=chunk, tb=tb),
        grid=(b, t // tb),
        in_specs=[
            tok(width), tok(M_WIDTH), tok(M_WIDTH), tok(LANES),
            pl.BlockSpec((1, tb // chunk, N_GATES, chunk), lambda i, j: (i, j, 0, 0)),
            _const_spec(conv_w.shape), _const_spec((1, width)), _const_spec(bc.shape), _const_spec(br.shape),
            _const_spec((1, M_WIDTH)),
            per_b((M_HEADS, M_HD, M_HD)), per_b((SUBLANES, M_HD)), per_b((SUBLANES, LANES)), per_b((SUBLANES, width)),
        ],
        out_specs=(tok(M_WIDTH), per_b((M_HEADS, M_HD, M_HD)), per_b((SUBLANES, M_HD)),
                   per_b((SUBLANES, LANES)), per_b((SUBLANES, width))),
        out_shape=out_shape,
        scratch_shapes=[
            pltpu.VMEM((SUBLANES + tb, width), F32),
            pltpu.VMEM((tb, width), BF16),
            pltpu.VMEM((M_HEADS, M_HD, M_HD), F32),
            pltpu.VMEM((SUBLANES, M_HD), F32),
            pltpu.VMEM((SUBLANES, LANES), F32),
        ],
        compiler_params=_params("parallel", "arbitrary"),
        name="mlstm",
    )(mqk, mv, mo, gc, gr, conv_w, conv_b.reshape(1, width), bc, br, norm_w.reshape(1, M_WIDTH), c0, n0, m0, cv0)


def _mix_kernel(x_ref, an_ref, mn_ref, sc_ref, sh_ref, g1_ref, wa_ref, wb_ref, wg_ref, bg_ref, wo_ref,
                lg_ref, lb_ref, o_ref, *, alpha):
    x = x_ref[0]
    h = (_layer_norm(x) * (1.0 + sc_ref[0]) + sh_ref[0]).astype(BF16)
    d = x.shape[-1]
    y_a = _dot(an_ref[0], wa_ref[...])
    y_b = _dot(mn_ref[0], wb_ref[...])
    g_a = jax.nn.sigmoid(_dot(h, wg_ref[:, :d]) + bg_ref[:, :d])
    g_b = jax.nn.sigmoid(_dot(h, wg_ref[:, d:]) + bg_ref[:, d:])
    mix = _dot((g_a * y_a + g_b * y_b).astype(BF16), wo_ref[...])
    o_ref[0] = _layer_norm(alpha * x + (1.0 + g1_ref[0]) * mix) * lg_ref[...] + lb_ref[...]


def _mix(x, an, mn, sc, sh, g1, w_a, w_b, w_g, b_g, w_o, ln_g, ln_b, *, tm, alpha):
    b, t, d = x.shape
    tok = lambda n: pl.BlockSpec((1, tm, n), lambda i, j: (i, j, 0))
    row = pl.BlockSpec((1, 1, d), lambda i, j: (i, 0, 0))
    return pl.pallas_call(
        functools.partial(_mix_kernel, alpha=alpha),
        grid=(b, t // tm),
        in_specs=[tok(d), tok(DA_WIDTH), tok(M_WIDTH), row, row, row,
                  _const_spec(w_a.shape), _const_spec(w_b.shape), _const_spec(w_g.shape), _const_spec((1, 2 * d)),
                  _const_spec(w_o.shape), _const_spec((1, d)), _const_spec((1, d))],
        out_specs=tok(d),
        out_shape=jax.ShapeDtypeStruct((b, t, d), F32),
        compiler_params=_params("parallel", "parallel"),
        name="mix_out",
    )(x, an, mn, sc, sh, g1, w_a, w_b, w_g, b_g.reshape(1, 2 * d), w_o, ln_g.reshape(1, d), ln_b.reshape(1, d))


def _ffn_kernel(x_ref, sc_ref, sh_ref, g2_ref, wgu_ref, wd_ref, lg_ref, lb_ref, o_ref, *, alpha, d_ff, fc):
    x = x_ref[0]
    h = (_layer_norm(x) * (1.0 + sc_ref[0]) + sh_ref[0]).astype(BF16)
    for j in range(d_ff // fc):
        gt = _dot(h, wgu_ref[:, j * fc:(j + 1) * fc])
        up = _dot(h, wgu_ref[:, d_ff + j * fc:d_ff + (j + 1) * fc])
        part = _dot((jax.nn.silu(gt) * up).astype(BF16), wd_ref[j * fc:(j + 1) * fc, :])
        if j == 0:
            o_ref[0] = part
        else:
            o_ref[0] += part
    o_ref[0] = _layer_norm(alpha * x_ref[0] + (1.0 + g2_ref[0]) * o_ref[0]) * lg_ref[...] + lb_ref[...]


def _ffn(x, sc, sh, g2, w_gu, w_down, ln_g, ln_b, *, tm, alpha):
    b, t, d = x.shape
    d_ff = w_down.shape[0]
    fc = 256
    tok = pl.BlockSpec((1, tm, d), lambda i, j: (i, j, 0))
    row = pl.BlockSpec((1, 1, d), lambda i, j: (i, 0, 0))
    return pl.pallas_call(
        functools.partial(_ffn_kernel, alpha=alpha, d_ff=d_ff, fc=fc),
        grid=(b, t // tm),
        in_specs=[tok, row, row, row, _const_spec(w_gu.shape), _const_spec(w_down.shape),
                  _const_spec((1, d)), _const_spec((1, d))],
        out_specs=tok,
        out_shape=jax.ShapeDtypeStruct((b, t, d), F32),
        compiler_params=_params("parallel", "parallel"),
        name="swiglu",
    )(x, sc, sh, g2, w_gu, w_down, ln_g.reshape(1, d), ln_b.reshape(1, d))


def _tile(t, pref):
    return pref if t % pref == 0 else t


def _layer(x, mod, layer, depth, attend, mstate, weights):
    (w_a, w_if, w_ift, b_if, conv_w, conv_b, lam_p, da_norm_w, m_norm_w, w_br_a, w_br_b, w_gate, b_gate, w_o,
     ln1_g, ln1_b, w_gu, w_down, ln2_g, ln2_b) = weights
    b, t, d = x.shape
    alpha = (2 * depth) ** 0.25
    lam_init = 0.8 - 0.6 * math.exp(-0.3 * layer)
    sh1, sc1, g1, sh2, sc2, g2 = (m.reshape(b, 1, d) for m in jnp.split(mod, 6, axis=-1))
    tm = _tile(t, 512)
    chunk = min(CHUNK, t)
    tb = _tile(t, 256)

    q, k, v, mqk, mv, mo, gc, gr = _inproj(x, sc1, sh1, w_a, w_if, w_ift, tm=tm, chunk=chunk)
    a_n = attend(q, k, v, lam_p, da_norm_w.reshape(1, DA_WIDTH), lam_init)
    m_n, c_f, n_f, m_f, cv_f = _mlstm(mqk, mv, mo, gc, gr, conv_w, conv_b, b_if, m_norm_w, *mstate, chunk=chunk, tb=tb)
    x = _mix(x, a_n, m_n, sc1, sh1, g1, w_br_a, w_br_b, w_gate, b_gate, w_o, ln1_g, ln1_b, tm=tm, alpha=alpha)
    x = _ffn(x, sc2, sh2, g2, w_gu, w_down, ln2_g, ln2_b, tm=tm, alpha=alpha)
    state = (c_f, n_f[:, :M_HEADS], m_f[:, :M_HEADS, 0], cv_f[:, SUBLANES - (CONV_W - 1):])
    return x, k.reshape(b, t, DA_HEADS, DA_QK), v.reshape(b, t, DA_HEADS, DA_VD), state


def _pad_rows(a, rows):
    pad = [(0, 0)] * a.ndim
    pad[1] = (rows - a.shape[1], 0)
    return jnp.pad(a, pad)


def kernel(x_prompt, x_sample, c_prompt, c_sample, cache_attn_k, cache_attn_v, state_mlstm_C, state_mlstm_n,
           state_mlstm_m, state_mlstm_conv, w_ada, b_ada, w_in, b_if, conv_w, conv_b, lam_p, da_norm_w, m_norm_w,
           w_br_a, w_br_b, w_gate, b_gate, w_o, ln1_g, ln1_b, w_gu, w_down, ln2_g, ln2_b):
    depth = w_in.shape[0]
    bp, bs = x_prompt.shape[0], x_sample.shape[0]
    past = cache_attn_k.shape[2]

    mod = _ada(jnp.concatenate([c_prompt, c_sample], axis=0), w_ada.astype(BF16), b_ada)

    gate_lo = 3 * DA_WIDTH + 2 * M_WIDTH + M_WIDTH
    gate_hi = gate_lo + N_GATES

    xp, xs = x_prompt, x_sample
    outs_p, outs_s = [], []
    for l in range(depth):
        w_l = w_in[l]
        w_a = jnp.concatenate([w_l[:, :gate_lo], w_l[:, gate_hi:]], axis=1).astype(BF16)
        w_if = jnp.pad(w_l[:, gate_lo:gate_hi], ((0, 0), (0, LANES - N_GATES))).astype(BF16)
        w_ift = w_l[:, gate_lo:gate_hi].T.astype(BF16)
        weights = (w_a, w_if, w_ift, b_if[l], conv_w[l], conv_b[l], lam_p[l], da_norm_w[l], m_norm_w[l],
                   w_br_a[l].astype(BF16), w_br_b[l].astype(BF16), w_gate[l].astype(BF16), b_gate[l],
                   w_o[l].astype(BF16), ln1_g[l], ln1_b[l], w_gu[l].astype(BF16), w_down[l].astype(BF16),
                   ln2_g[l], ln2_b[l])

        zero_state = (jnp.zeros((bp, M_HEADS, M_HD, M_HD), F32), jnp.zeros((bp, SUBLANES, M_HD), F32),
                      jnp.zeros((bp, SUBLANES, LANES), F32), jnp.zeros((bp, SUBLANES, 2 * M_WIDTH), F32))
        attend_p = lambda q, k, v, lp, g, lam_init: _attn_prompt(q, k, v, lp, g, lam_init=lam_init,
                                                                 tq=_tile(q.shape[1], 256))
        xp, k_p, v_p, st_p = _layer(xp, mod[l, :bp], l, depth, attend_p, zero_state, weights)
        outs_p.append((k_p, v_p) + st_p)

        ck = cache_attn_k[l].reshape(bs, past, DA_WIDTH)
        cv = cache_attn_v[l].reshape(bs, past, DA_WIDTH)
        attend_s = lambda q, k, v, lp, g, lam_init, ck=ck, cv=cv: _attn_sample(q, k, v, ck, cv, lp, g, lam_init=lam_init)
        head_pad = ((0, 0), (0, SUBLANES - M_HEADS), (0, 0))
        state_s = (state_mlstm_C[l],
                   jnp.pad(state_mlstm_n[l], head_pad),
                   jnp.pad(jnp.broadcast_to(state_mlstm_m[l][:, :, None], (bs, M_HEADS, LANES)), head_pad),
                   _pad_rows(state_mlstm_conv[l], SUBLANES))
        xs, k_s, v_s, st_s = _layer(xs, mod[l, bp:], l, depth, attend_s, state_s, weights)
        outs_s.append((k_s, v_s) + st_s)

    stack = lambda outs, i: jnp.stack([o[i] for o in outs])
    kp, vp, cp, np_, mp, cvp = (stack(outs_p, i) for i in range(6))
    ks, vs, cs, ns, ms, cvs = (stack(outs_s, i) for i in range(6))
    return (xp, xs, kp, vp, ks, vs, cp, np_, mp, cvp, cs, ns, ms, cvs)
```

```python
import functools
import math

import jax
import jax.numpy as jnp
from jax import lax
from jax.experimental import pallas as pl
from jax.experimental.pallas import tpu as pltpu

F32 = jnp.float32
BF16 = jnp.bfloat16

D_MODEL = 1024
CHUNK = 64
DA_HEADS = 4
DA_HD = 64
DA_QK = 2 * DA_HD
DA_VD = 2 * DA_HD
DA_WIDTH = DA_HEADS * DA_VD
M_HEADS = 4
M_HD = 128
M_WIDTH = M_HEADS * M_HD
CONV_W = 4
N_GATES = 2 * M_HEADS
LN_EPS = 1e-5
LOG2E = math.log2(math.e)
Q_SCALE = DA_HD ** -0.5 * LOG2E
ALIBI_SLOPES = tuple(2.0 ** (-8.0 * (i + 1) / DA_HEADS) for i in range(DA_HEADS))

SUBLANES = 8
LANES = 128
VMEM_LIMIT_BYTES = 56 * 1024 * 1024

_NT = (((1,), (1,)), ((), ()))
_TN = (((0,), (0,)), ((), ()))


def _dot(a, b):
    return jnp.dot(a, b, preferred_element_type=F32)


def _dot_nt(a, b):
    return lax.dot_general(a, b, _NT, preferred_element_type=F32)


def _dot_tn(a, b):
    return lax.dot_general(a, b, _TN, preferred_element_type=F32)


def _layer_norm(x):
    mu = jnp.mean(x, axis=-1, keepdims=True)
    xc = x - mu
    var = jnp.mean(xc * xc, axis=-1, keepdims=True)
    return xc * lax.rsqrt(var + LN_EPS)


def _params(*sem):
    return pltpu.CompilerParams(dimension_semantics=sem, vmem_limit_bytes=VMEM_LIMIT_BYTES)


def _const_spec(shape):
    nd = len(shape)
    return pl.BlockSpec(shape, lambda *_: (0,) * nd, pipeline_mode=pl.Buffered(1))


def _ada_kernel(c_ref, w_ref, b_ref, o_ref):
    s = jax.nn.silu(c_ref[...]).astype(BF16)
    o_ref[0] = _dot(s, w_ref[0]) + b_ref[0]


def _ada(c_all, w_ada, b_ada):
    depth, d, n = w_ada.shape
    r = c_all.shape[0]
    tn = 2048
    return pl.pallas_call(
        _ada_kernel,
        grid=(depth, n // tn),
        in_specs=[
            pl.BlockSpec((r, d), lambda l, j: (0, 0)),
            pl.BlockSpec((1, d, tn), lambda l, j: (l, 0, j)),
            pl.BlockSpec((1, 1, tn), lambda l, j: (l, 0, j)),
        ],
        out_specs=pl.BlockSpec((1, r, tn), lambda l, j: (l, 0, j)),
        out_shape=jax.ShapeDtypeStruct((depth, r, n), F32),
        compiler_params=_params("parallel", "parallel"),
        name="ada_mod",
    )(c_all, w_ada, b_ada.reshape(depth, 1, n))


def _inproj_kernel(x_ref, sc_ref, sh_ref, w_ref, wif_ref, wift_ref,
                   q_ref, k_ref, v_ref, mqk_ref, mv_ref, mo_ref, gc_ref, gr_ref, *, n_chunks, chunk):
    x = x_ref[0]
    h = (_layer_norm(x) * (1.0 + sc_ref[0]) + sh_ref[0]).astype(BF16)

    def proj(lo, hi):
        return _dot(h, w_ref[:, lo:hi])

    q_ref[0] = (proj(0, 512) * Q_SCALE).astype(BF16)
    k_ref[0] = proj(512, 1024)
    v_ref[0] = proj(1024, 1536)
    mqk_ref[0] = proj(1536, 2560)
    mv_ref[0] = proj(2560, 3072).astype(BF16)
    mo_ref[0] = proj(3072, 3584)
    gc_ref[0] = _dot(h, wif_ref[...])
    gr = _dot_nt(wift_ref[...], h)
    for j in range(n_chunks):
        gr_ref[0, j] = gr[:, j * chunk:(j + 1) * chunk]


def _inproj(x, sc, sh, w_a, w_if, w_ift, *, tm, chunk):
    b, t, d = x.shape
    n_chunks = tm // chunk
    tok = lambda n: pl.BlockSpec((1, tm, n), lambda i, j: (i, j, 0))
    row = pl.BlockSpec((1, 1, d), lambda i, j: (i, 0, 0))
    out_shape = (
        jax.ShapeDtypeStruct((b, t, DA_WIDTH), BF16),
        jax.ShapeDtypeStruct((b, t, DA_WIDTH), F32),
        jax.ShapeDtypeStruct((b, t, DA_WIDTH), F32),
        jax.ShapeDtypeStruct((b, t, 2 * M_WIDTH), F32),
        jax.ShapeDtypeStruct((b, t, M_WIDTH), BF16),
        jax.ShapeDtypeStruct((b, t, M_WIDTH), F32),
        jax.ShapeDtypeStruct((b, t, LANES), F32),
        jax.ShapeDtypeStruct((b, t // chunk, N_GATES, chunk), F32),
    )
    out_specs = (
        tok(DA_WIDTH), tok(DA_WIDTH), tok(DA_WIDTH), tok(2 * M_WIDTH), tok(M_WIDTH), tok(M_WIDTH), tok(LANES),
        pl.BlockSpec((1, n_chunks, N_GATES, chunk), lambda i, j: (i, j, 0, 0)),
    )
    return pl.pallas_call(
        functools.partial(_inproj_kernel, n_chunks=n_chunks, chunk=chunk),
        grid=(b, t // tm),
        in_specs=[tok(d), row, row, _const_spec(w_a.shape), _const_spec(w_if.shape), _const_spec(w_ift.shape)],
        out_specs=out_specs,
        out_shape=out_shape,
        compiler_params=_params("parallel", "parallel"),
        name="in_proj",
    )(x, sc, sh, w_a, w_if, w_ift)


def _lambda(lp_ref, lam_init):
    lp = lp_ref[...]
    a = jnp.sum(lp[0:1] * lp[1:2], axis=1, keepdims=True)
    b = jnp.sum(lp[2:3] * lp[3:4], axis=1, keepdims=True)
    return jnp.exp(a) - jnp.exp(b) + lam_init


def _alibi_slope(h):
    s = ALIBI_SLOPES
    return jnp.where(h == 0, s[0], jnp.where(h == 1, s[1], jnp.where(h == 2, s[2], s[3]))).astype(F32)


def _attn_prompt_kernel(q_ref, k_ref, v_ref, lp_ref, g_ref, o_ref,
                        kb, vt, nrel, qs, sc, acc, mpart, lpart, *, tq, t, lam_init):
    qi = pl.program_id(1)
    tk = tq
    coef = [s * LOG2E for s in ALIBI_SLOPES]

    @pl.when(qi == 0)
    def _():
        for c in range(t // tk):
            rows = slice(c * tk, (c + 1) * tk)
            kb[rows, :] = k_ref[0, rows, :].astype(BF16)
            vt[:, rows] = v_ref[0, rows, :].T.astype(BF16)
        rel = (lax.broadcasted_iota(jnp.int32, (tk, tq), 1) - lax.broadcasted_iota(jnp.int32, (tk, tq), 0)).astype(F32)
        for h in range(DA_HEADS):
            nrel[h] = -coef[h] * rel

    lane = lax.broadcasted_iota(jnp.int32, (tq, DA_QK), 1)
    zero = jnp.zeros((tq, DA_QK), BF16)
    for h in range(DA_HEADS):
        qh = q_ref[0, :, h * DA_QK:(h + 1) * DA_QK]
        qs[2 * h] = jnp.where(lane < DA_HD, qh, zero)
        qs[2 * h + 1] = jnp.where(lane >= DA_HD, qh, zero)

    mpart[...] = jnp.full(mpart.shape, -jnp.inf, F32)
    groups = tk // SUBLANES

    def shift(kj, h):
        return (-coef[h] * tq) * (qi - kj).astype(F32)

    def scores(kj, masked):
        off = pl.multiple_of(kj * tk, tk)
        if masked:
            krow = lax.broadcasted_iota(jnp.int32, (tk, tq), 0)
            qcol = lax.broadcasted_iota(jnp.int32, (tk, tq), 1)
            visible = (krow // CHUNK) <= (qcol // CHUNK)
        for h in range(DA_HEADS):
            kblk = kb[pl.ds(off, tk), h * DA_QK:(h + 1) * DA_QK]
            bias = nrel[h]
            if masked:
                bias = jnp.minimum(bias, -bias)
            for c in range(2):
                i = 2 * h + c
                s = _dot_nt(kblk, qs[i]) + bias
                if masked:
                    s = jnp.where(visible, s, -jnp.inf)
                sc[i, pl.ds(off, tk), :] = s
                blk_max = jnp.max(s.reshape(groups, SUBLANES, tq), axis=0)
                mpart[i] = jnp.maximum(mpart[i], blk_max if masked else blk_max + shift(kj, h))

    def scores_body(kj, carry):
        scores(kj, masked=False)
        return carry

    lax.fori_loop(0, qi, scores_body, 0)
    scores(qi, masked=True)

    for i in range(2 * DA_HEADS):
        mpart[i] = jnp.broadcast_to(jnp.max(mpart[i], axis=0, keepdims=True), (SUBLANES, tq))
    lpart[...] = jnp.zeros(lpart.shape, F32)
    acc[...] = jnp.zeros(acc.shape, F32)

    def weights_body(kj, carry):
        off = pl.multiple_of(kj * tk, tk)
        for h in range(DA_HEADS):
            vblk = vt[h * DA_VD:(h + 1) * DA_VD, pl.ds(off, tk)]
            for c in range(2):
                i = 2 * h + c
                s = sc[i, pl.ds(off, tk), :].reshape(groups, SUBLANES, tq)
                p = jnp.exp2(s - (mpart[i] - shift(kj, h))[None])
                lpart[i] += jnp.sum(p, axis=0)
                acc[i] += _dot(vblk, p.reshape(tk, tq).astype(BF16))
        return carry

    lax.fori_loop(0, qi + 1, weights_body, 0)

    lam = _lambda(lp_ref, lam_init)
    for h in range(DA_HEADS):
        hc = slice(h * DA_VD, (h + 1) * DA_VD)
        l0 = jnp.sum(lpart[2 * h], axis=0, keepdims=True)
        l1 = jnp.sum(lpart[2 * h + 1], axis=0, keepdims=True)
        o = acc[2 * h] / l0 - lam * (acc[2 * h + 1] / l1)
        o = o * lax.rsqrt(jnp.mean(o * o, axis=0, keepdims=True) + LN_EPS)
        o_ref[0, :, hc] = (o.T * g_ref[:, hc] * (1.0 - lam_init)).astype(BF16)


def _attn_prompt(q, k, v, lam_p, gain, *, lam_init, tq):
    b, t, _ = q.shape
    kv = pl.BlockSpec((1, t, DA_WIDTH), lambda i, j: (i, 0, 0))
    tok = pl.BlockSpec((1, tq, DA_WIDTH), lambda i, j: (i, j, 0))
    return pl.pallas_call(
        functools.partial(_attn_prompt_kernel, tq=tq, t=t, lam_init=lam_init),
        grid=(b, t // tq),
        in_specs=[tok, kv, kv, _const_spec(lam_p.shape), _const_spec(gain.shape)],
        out_specs=tok,
        out_shape=jax.ShapeDtypeStruct((b, t, DA_WIDTH), BF16),
        scratch_shapes=[
            pltpu.VMEM((t, DA_WIDTH), BF16),
            pltpu.VMEM((DA_WIDTH, t), BF16),
            pltpu.VMEM((DA_HEADS, tq, tq), F32),
            pltpu.VMEM((2 * DA_HEADS, tq, DA_QK), BF16),
            pltpu.VMEM((2 * DA_HEADS, t, tq), F32),
            pltpu.VMEM((2 * DA_HEADS, DA_VD, tq), F32),
            pltpu.VMEM((2 * DA_HEADS, SUBLANES, tq), F32),
            pltpu.VMEM((2 * DA_HEADS, SUBLANES, tq), F32),
        ],
        compiler_params=_params("parallel", "arbitrary"),
        name="diff_attn_prompt",
    )(q, k, v, lam_p, gain)


def _attn_sample_kernel(q_ref, k_ref, v_ref, ck_ref, cv_ref, lp_ref, g_ref, o_ref, *, past, ts, lam_init):
    h = pl.program_id(1)
    q = q_ref[0]
    lane = lax.broadcasted_iota(jnp.int32, q.shape, 1)
    zero = jnp.zeros_like(q)
    qc = (jnp.where(lane < DA_HD, q, zero), jnp.where(lane >= DA_HD, q, zero))
    slope = _alibi_slope(h)

    def bias_mask(n_keys, key_base):
        qpos = past + lax.broadcasted_iota(jnp.int32, (ts, n_keys), 0)
        kpos = key_base + lax.broadcasted_iota(jnp.int32, (ts, n_keys), 1)
        bias = (-LOG2E * slope) * jnp.abs(qpos - kpos).astype(F32)
        return bias, (kpos // CHUNK) <= (qpos // CHUNK)

    ck = ck_ref[0].astype(BF16)
    cv = cv_ref[0].astype(BF16)
    kn = k_ref[0].astype(BF16)
    vn = v_ref[0].astype(BF16)
    bias_c, vis_c = bias_mask(past, 0)
    bias_n, vis_n = bias_mask(ts, past)
    outs = []
    for c in range(2):
        s_c = jnp.where(vis_c, _dot_nt(qc[c], ck) + bias_c, -jnp.inf)
        s_n = jnp.where(vis_n, _dot_nt(qc[c], kn) + bias_n, -jnp.inf)
        m = jnp.maximum(jnp.max(s_c, axis=1, keepdims=True), jnp.max(s_n, axis=1, keepdims=True))
        p_c = jnp.exp2(s_c - m)
        p_n = jnp.exp2(s_n - m)
        l = jnp.sum(p_c, axis=1, keepdims=True) + jnp.sum(p_n, axis=1, keepdims=True)
        outs.append((_dot(p_c.astype(BF16), cv) + _dot(p_n.astype(BF16), vn)) / l)
    lam = _lambda(lp_ref, lam_init)
    o = outs[0] - lam * outs[1]
    o = o * lax.rsqrt(jnp.mean(o * o, axis=1, keepdims=True) + LN_EPS)
    o_ref[0] = (o * g_ref[...] * (1.0 - lam_init)).astype(BF16)


def _attn_sample(q, k, v, cache_k, cache_v, lam_p, gain, *, lam_init):
    b, ts, _ = q.shape
    past = cache_k.shape[1]
    new = pl.BlockSpec((1, ts, DA_QK), lambda i, h: (i, 0, h))
    old = pl.BlockSpec((1, past, DA_QK), lambda i, h: (i, 0, h))
    return pl.pallas_call(
        functools.partial(_attn_sample_kernel, past=past, ts=ts, lam_init=lam_init),
        grid=(b, DA_HEADS),
        in_specs=[new, new, new, old, old,
                  pl.BlockSpec(lam_p.shape, lambda i, h: (0, 0)),
                  pl.BlockSpec((1, DA_VD), lambda i, h: (0, h))],
        out_specs=new,
        out_shape=jax.ShapeDtypeStruct((b, ts, DA_WIDTH), BF16),
        compiler_params=_params("parallel", "parallel"),
        name="diff_attn_sample",
    )(q, k, v, cache_k, cache_v, lam_p, gain)


def _split3(x):
    hi = x.astype(BF16)
    r = x - hi.astype(F32)
    mid = r.astype(BF16)
    lo = (r - mid.astype(F32)).astype(BF16)
    return hi, mid, lo


def _mlstm_kernel(mqk_ref, mv_ref, mo_ref, gc_ref, gr_ref, cw_ref, cb_ref, bc_ref, br_ref, nw_ref,
                  c0_ref, n0_ref, m0_ref, cv0_ref,
                  o_ref, cf_ref, nf_ref, mf_ref, cvf_ref,
                  ubuf, qk_s, c_s, n_s, m_s, *, chunk, tb):
    t = pl.program_id(1)
    n_t = pl.num_programs(1)
    width = 2 * M_WIDTH

    @pl.when(t == 0)
    def _():
        ubuf[0:SUBLANES, :] = cv0_ref[0]
        c_s[...] = c0_ref[0]
        n_s[...] = n0_ref[0]
        m_s[...] = m0_ref[0]

    ubuf[SUBLANES:SUBLANES + tb, :] = mqk_ref[0]
    k_scale = M_HD ** -0.5
    for j in range(width // LANES):
        cols = slice(j * LANES, (j + 1) * LANES)
        y = cb_ref[:, cols]
        for tap in range(CONV_W):
            start = SUBLANES - (CONV_W - 1) + tap
            y = y + ubuf[start:start + tb, cols] * cw_ref[tap:tap + 1, cols]
        y = jax.nn.silu(y)
        if j >= M_WIDTH // LANES:
            y = y * k_scale
        qk_s[:, cols] = y.astype(BF16)
    ubuf[0:SUBLANES, :] = ubuf[tb:tb + SUBLANES, :]

    row_i = lax.broadcasted_iota(jnp.int32, (chunk, chunk), 0)
    col_i = lax.broadcasted_iota(jnp.int32, (chunk, chunk), 1)
    tril = (col_i <= row_i).astype(BF16)
    triu = (row_i <= col_i).astype(BF16)
    seen = row_i <= col_i
    last = chunk - 1

    for ci in range(tb // chunk):
        rows = slice(ci * chunk, (ci + 1) * chunk)
        gcb = gc_ref[0, rows, :] + bc_ref[...]
        grb = gr_ref[0, ci] + br_ref[...]
        b_cols = sum(_dot(tril, part) for part in _split3(jax.nn.log_sigmoid(gcb)))
        b_rows = sum(_dot(part, triu) for part in _split3(jax.nn.log_sigmoid(grb)))
        g_cols = gcb - pltpu.roll(b_cols, LANES - M_HEADS, 1)
        n_bf = n_s[...].astype(BF16)

        for h in range(M_HEADS):
            hc = slice(h * M_HD, (h + 1) * M_HD)
            q = qk_s[rows, hc]
            k = qk_s[rows, M_WIDTH + h * M_HD:M_WIDTH + (h + 1) * M_HD]
            v = mv_ref[0, rows, hc]
            b_row = b_rows[M_HEADS + h:M_HEADS + h + 1, :]
            m_prev = m_s[h:h + 1, 0:1]
            c_prev = c_s[h]
            n_prev = n_s[h:h + 1, :]

            g_b = jnp.broadcast_to(g_cols[:, h:h + 1], (chunk, LANES))
            g_m = jnp.where(seen, g_b[:, :chunk], -jnp.inf)
            g_max = jnp.max(g_m, axis=0, keepdims=True)
            p_t = _dot_nt(k, q) * jnp.exp(g_m - g_max)
            r = jnp.sum(p_t, axis=0, keepdims=True)
            u_t = _dot_tn(v, p_t.astype(BF16))
            g_last = g_max[:, last:]
            ws_b = jnp.exp(g_b - g_last)
            dc = _dot_tn((v.astype(F32) * ws_b).astype(BF16), k)
            dn = jnp.sum(ws_b * k.astype(F32), axis=0, keepdims=True)

            x = g_max - m_prev
            e1 = jnp.exp(jnp.minimum(x, 0.0))
            e2 = jnp.exp(jnp.minimum(-x, 0.0))
            m_t = b_row + jnp.maximum(m_prev, g_max)
            qc_t = _dot_nt(c_prev.astype(BF16), q)
            qn = _dot_nt(n_bf, q)[h:h + 1, :]
            den = e1 * r + e2 * qn
            h_t = (e1 * u_t + e2 * qc_t) * (1.0 / jnp.maximum(jnp.abs(den), jnp.exp(-m_t)))
            f, wc = e1[:, last:], e2[:, last:]
            c_s[h] = wc * c_prev + f * dc
            n_s[h:h + 1, :] = wc * n_prev + f * dn
            m_s[h:h + 1, :] = jnp.broadcast_to(b_row[:, last:] + jnp.maximum(m_prev, g_last), (1, LANES))

            hcen = h_t - jnp.mean(h_t, axis=0, keepdims=True)
            hn = (hcen * lax.rsqrt(jnp.mean(hcen * hcen, axis=0, keepdims=True) + LN_EPS)).T
            o_ref[0, rows, hc] = (hn * nw_ref[:, hc] * jax.nn.sigmoid(mo_ref[0, rows, hc])).astype(BF16)

    @pl.when(t == n_t - 1)
    def _():
        cf_ref[0] = c_s[...]
        nf_ref[0] = n_s[...]
        mf_ref[0] = m_s[...]
        cvf_ref[0] = ubuf[0:SUBLANES, :]


def _mlstm(mqk, mv, mo, gc, gr, conv_w, conv_b, b_if, norm_w, c0, n0, m0, cv0, *, chunk, tb):
    b, t, width = mqk.shape
    tok = lambda n: pl.BlockSpec((1, tb, n), lambda i, j: (i, j, 0))
    per_b = lambda shape: pl.BlockSpec((1,) + shape, lambda i, j: (i,) + (0,) * len(shape))
    bc = jnp.zeros((1, LANES), F32).at[0, :N_GATES].set(b_if)
    br = b_if.reshape(N_GATES, 1)
    out_shape = (
        jax.ShapeDtypeStruct((b, t, M_WIDTH), BF16),
        jax.ShapeDtypeStruct((b, M_HEADS, M_HD, M_HD), F32),
        jax.ShapeDtypeStruct((b, SUBLANES, M_HD), F32),
        jax.ShapeDtypeStruct((b, SUBLANES, LANES), F32),
        jax.ShapeDtypeStruct((b, SUBLANES, width), F32),
    )
    return pl.pallas_call(
        functools.partial(_mlstm_kernel, chunk=chunk, tb=tb),
        grid=(b, t // tb),
        in_specs=[
            tok(width), tok(M_WIDTH), tok(M_WIDTH), tok(LANES),
            pl.BlockSpec((1, tb // chunk, N_GATES, chunk), lambda i, j: (i, j, 0, 0)),
            _const_spec(conv_w.shape), _const_spec((1, width)), _const_spec(bc.shape), _const_spec(br.shape),
            _const_spec((1, M_WIDTH)),
            per_b((M_HEADS, M_HD, M_HD)), per_b((SUBLANES, M_HD)), per_b((SUBLANES, LANES)), per_b((SUBLANES, width)),
        ],
        out_specs=(tok(M_WIDTH), per_b((M_HEADS, M_HD, M_HD)), per_b((SUBLANES, M_HD)),
                   per_b((SUBLANES, LANES)), per_b((SUBLANES, width))),
        out_shape=out_shape,
        scratch_shapes=[
            pltpu.VMEM((SUBLANES + tb, width), F32),
            pltpu.VMEM((tb, width), BF16),
            pltpu.VMEM((M_HEADS, M_HD, M_HD), F32),
            pltpu.VMEM((SUBLANES, M_HD), F32),
            pltpu.VMEM((SUBLANES, LANES), F32),
        ],
        compiler_params=_params("parallel", "arbitrary"),
        name="mlstm",
    )(mqk, mv, mo, gc, gr, conv_w, conv_b.reshape(1, width), bc, br, norm_w.reshape(1, M_WIDTH), c0, n0, m0, cv0)


def _mix_kernel(x_ref, an_ref, mn_ref, sc_ref, sh_ref, g1_ref, wa_ref, wb_ref, wg_ref, bg_ref, wo_ref,
                lg_ref, lb_ref, o_ref, *, alpha):
    x = x_ref[0]
    h = (_layer_norm(x) * (1.0 + sc_ref[0]) + sh_ref[0]).astype(BF16)
    d = x.shape[-1]
    y_a = _dot(an_ref[0], wa_ref[...])
    y_b = _dot(mn_ref[0], wb_ref[...])
    g_a = jax.nn.sigmoid(_dot(h, wg_ref[:, :d]) + bg_ref[:, :d])
    g_b = jax.nn.sigmoid(_dot(h, wg_ref[:, d:]) + bg_ref[:, d:])
    mix = _dot((g_a * y_a + g_b * y_b).astype(BF16), wo_ref[...])
    o_ref[0] = _layer_norm(alpha * x + (1.0 + g1_ref[0]) * mix) * lg_ref[...] + lb_ref[...]


def _mix(x, an, mn, sc, sh, g1, w_a, w_b, w_g, b_g, w_o, ln_g, ln_b, *, tm, alpha):
    b, t, d = x.shape
    tok = lambda n: pl.BlockSpec((1, tm, n), lambda i, j: (i, j, 0))
    row = pl.BlockSpec((1, 1, d), lambda i, j: (i, 0, 0))
    return pl.pallas_call(
        functools.partial(_mix_kernel, alpha=alpha),
        grid=(b, t // tm),
        in_specs=[tok(d), tok(DA_WIDTH), tok(M_WIDTH), row, row, row,
                  _const_spec(w_a.shape), _const_spec(w_b.shape), _const_spec(w_g.shape), _const_spec((1, 2 * d)),
                  _const_spec(w_o.shape), _const_spec((1, d)), _const_spec((1, d))],
        out_specs=tok(d),
        out_shape=jax.ShapeDtypeStruct((b, t, d), F32),
        compiler_params=_params("parallel", "parallel"),
        name="mix_out",
    )(x, an, mn, sc, sh, g1, w_a, w_b, w_g, b_g.reshape(1, 2 * d), w_o, ln_g.reshape(1, d), ln_b.reshape(1, d))


def _ffn_kernel(x_ref, sc_ref, sh_ref, g2_ref, wgu_ref, wd_ref, lg_ref, lb_ref, o_ref, *, alpha, d_ff, fc):
    x = x_ref[0]
    h = (_layer_norm(x) * (1.0 + sc_ref[0]) + sh_ref[0]).astype(BF16)
    for j in range(d_ff // fc):
        gt = _dot(h, wgu_ref[:, j * fc:(j + 1) * fc])
        up = _dot(h, wgu_ref[:, d_ff + j * fc:d_ff + (j + 1) * fc])
        part = _dot((jax.nn.silu(gt) * up).astype(BF16), wd_ref[j * fc:(j + 1) * fc, :])
        if j == 0:
            o_ref[0] = part
        else:
            o_ref[0] += part
    o_ref[0] = _layer_norm(alpha * x_ref[0] + (1.0 + g2_ref[0]) * o_ref[0]) * lg_ref[...] + lb_ref[...]


def _ffn(x, sc, sh, g2, w_gu, w_down, ln_g, ln_b, *, tm, alpha):
    b, t, d = x.shape
    d_ff = w_down.shape[0]
    fc = 256
    tok = pl.BlockSpec((1, tm, d), lambda i, j: (i, j, 0))
    row = pl.BlockSpec((1, 1, d), lambda i, j: (i, 0, 0))
    return pl.pallas_call(
        functools.partial(_ffn_kernel, alpha=alpha, d_ff=d_ff, fc=fc),
        grid=(b, t // tm),
        in_specs=[tok, row, row, row, _const_spec(w_gu.shape), _const_spec(w_down.shape),
                  _const_spec((1, d)), _const_spec((1, d))],
        out_specs=tok,
        out_shape=jax.ShapeDtypeStruct((b, t, d), F32),
        compiler_params=_params("parallel", "parallel"),
        name="swiglu",
    )(x, sc, sh, g2, w_gu, w_down, ln_g.reshape(1, d), ln_b.reshape(1, d))


def _tile(t, pref):
    return pref if t % pref == 0 else t


def _layer(x, mod, layer, depth, attend, mstate, weights):
    (w_a, w_if, w_ift, b_if, conv_w, conv_b, lam_p, da_norm_w, m_norm_w, w_br_a, w_br_b, w_gate, b_gate, w_o,
     ln1_g, ln1_b, w_gu, w_down, ln2_g, ln2_b) = weights
    b, t, d = x.shape
    alpha = (2 * depth) ** 0.25
    lam_init = 0.8 - 0.6 * math.exp(-0.3 * layer)
    sh1, sc1, g1, sh2, sc2, g2 = (m.reshape(b, 1, d) for m in jnp.split(mod, 6, axis=-1))
    tm = _tile(t, 512)
    chunk = min(CHUNK, t)
    tb = _tile(t, 256)

    q, k, v, mqk, mv, mo, gc, gr = _inproj(x, sc1, sh1, w_a, w_if, w_ift, tm=tm, chunk=chunk)
    a_n = attend(q, k, v, lam_p, da_norm_w.reshape(1, DA_WIDTH), lam_init)
    m_n, c_f, n_f, m_f, cv_f = _mlstm(mqk, mv, mo, gc, gr, conv_w, conv_b, b_if, m_norm_w, *mstate, chunk=chunk, tb=tb)
    x = _mix(x, a_n, m_n, sc1, sh1, g1, w_br_a, w_br_b, w_gate, b_gate, w_o, ln1_g, ln1_b, tm=tm, alpha=alpha)
    x = _ffn(x, sc2, sh2, g2, w_gu, w_down, ln2_g, ln2_b, tm=tm, alpha=alpha)
    state = (c_f, n_f[:, :M_HEADS], m_f[:, :M_HEADS, 0], cv_f[:, SUBLANES - (CONV_W - 1):])
    return x, k.reshape(b, t, DA_HEADS, DA_QK), v.reshape(b, t, DA_HEADS, DA_VD), state


def _pad_rows(a, rows):
    pad = [(0, 0)] * a.ndim
    pad[1] = (rows - a.shape[1], 0)
    return jnp.pad(a, pad)


def kernel(x_prompt, x_sample, c_prompt, c_sample, cache_attn_k, cache_attn_v, state_mlstm_C, state_mlstm_n,
           state_mlstm_m, state_mlstm_conv, w_ada, b_ada, w_in, b_if, conv_w, conv_b, lam_p, da_norm_w, m_norm_w,
           w_br_a, w_br_b, w_gate, b_gate, w_o, ln1_g, ln1_b, w_gu, w_down, ln2_g, ln2_b):
    depth = w_in.shape[0]
    bp, bs = x_prompt.shape[0], x_sample.shape[0]
    past = cache_attn_k.shape[2]

    mod = _ada(jnp.concatenate([c_prompt, c_sample], axis=0), w_ada.astype(BF16), b_ada)

    gate_lo = 3 * DA_WIDTH + 2 * M_WIDTH + M_WIDTH
    gate_hi = gate_lo + N_GATES

    xp, xs = x_prompt, x_sample
    outs_p, outs_s = [], []
    for l in range(depth):
        w_l = w_in[l]
        w_a = jnp.concatenate([w_l[:, :gate_lo], w_l[:, gate_hi:]], axis=1).astype(BF16)
        w_if = jnp.pad(w_l[:, gate_lo:gate_hi], ((0, 0), (0, LANES - N_GATES))).astype(BF16)
        w_ift = w_l[:, gate_lo:gate_hi].T.astype(BF16)
        weights = (w_a, w_if, w_ift, b_if[l], conv_w[l], conv_b[l], lam_p[l], da_norm_w[l], m_norm_w[l],
                   w_br_a[l].astype(BF16), w_br_b[l].astype(BF16), w_gate[l].astype(BF16), b_gate[l],
                   w_o[l].astype(BF16), ln1_g[l], ln1_b[l], w_gu[l].astype(BF16), w_down[l].astype(BF16),
                   ln2_g[l], ln2_b[l])

        zero_state = (jnp.zeros((bp, M_HEADS, M_HD, M_HD), F32), jnp.zeros((bp, SUBLANES, M_HD), F32),
                      jnp.zeros((bp, SUBLANES, LANES), F32), jnp.zeros((bp, SUBLANES, 2 * M_WIDTH), F32))
        attend_p = lambda q, k, v, lp, g, lam_init: _attn_prompt(q, k, v, lp, g, lam_init=lam_init,
                                                                 tq=_tile(q.shape[1], 256))
        xp, k_p, v_p, st_p = _layer(xp, mod[l, :bp], l, depth, attend_p, zero_state, weights)
        outs_p.append((k_p, v_p) + st_p)

        ck = cache_attn_k[l].reshape(bs, past, DA_WIDTH)
        cv = cache_attn_v[l].reshape(bs, past, DA_WIDTH)
        attend_s = lambda q, k, v, lp, g, lam_init, ck=ck, cv=cv: _attn_sample(q, k, v, ck, cv, lp, g, lam_init=lam_init)
        head_pad = ((0, 0), (0, SUBLANES - M_HEADS), (0, 0))
        state_s = (state_mlstm_C[l],
                   jnp.pad(state_mlstm_n[l], head_pad),
                   jnp.pad(jnp.broadcast_to(state_mlstm_m[l][:, :, None], (bs, M_HEADS, LANES)), head_pad),
                   _pad_rows(state_mlstm_conv[l], SUBLANES))
        xs, k_s, v_s, st_s = _layer(xs, mod[l, bp:], l, depth, attend_s, state_s, weights)
        outs_s.append((k_s, v_s) + st_s)

    stack = lambda outs, i: jnp.stack([o[i] for o in outs])
    kp, vp, cp, np_, mp, cvp = (stack(outs_p, i) for i in range(6))
    ks, vs, cs, ns, ms, cvs = (stack(outs_s, i) for i in range(6))
    return (xp, xs, kp, vp, ks, vs, cp, np_, mp, cvp, cs, ns, ms, cvs)
```

```python
import functools
import math

import jax
import jax.numpy as jnp
from jax import lax
from jax.experimental import pallas as pl
from jax.experimental.pallas import tpu as pltpu

F32 = jnp.float32
BF16 = jnp.bfloat16

D_MODEL = 1024
CHUNK = 64
DA_HEADS = 4
DA_HD = 64
DA_QK = 2 * DA_HD
DA_VD = 2 * DA_HD
DA_WIDTH = DA_HEADS * DA_VD
M_HEADS = 4
M_HD = 128
M_WIDTH = M_HEADS * M_HD
CONV_W = 4
N_GATES = 2 * M_HEADS
LN_EPS = 1e-5
LOG2E = math.log2(math.e)
Q_SCALE = DA_HD ** -0.5 * LOG2E
ALIBI_SLOPES = tuple(2.0 ** (-8.0 * (i + 1) / DA_HEADS) for i in range(DA_HEADS))

SUBLANES = 8
LANES = 128
VMEM_LIMIT_BYTES = 56 * 1024 * 1024

_NT = (((1,), (1,)), ((), ()))
_TN = (((0,), (0,)), ((), ()))


def _dot(a, b):
    return jnp.dot(a, b, preferred_element_type=F32)


def _dot_nt(a, b):
    return lax.dot_general(a, b, _NT, preferred_element_type=F32)


def _dot_tn(a, b):
    return lax.dot_general(a, b, _TN, preferred_element_type=F32)


def _layer_norm(x):
    mu = jnp.mean(x, axis=-1, keepdims=True)
    xc = x - mu
    var = jnp.mean(xc * xc, axis=-1, keepdims=True)
    return xc * lax.rsqrt(var + LN_EPS)


ROW_PARTS = 2


def _row_parts(rows):
    if rows % (ROW_PARTS * 128) != 0:
        return (slice(0, rows),)
    part = rows // ROW_PARTS
    return tuple(slice(i * part, (i + 1) * part) for i in range(ROW_PARTS))


def _params(*sem):
    return pltpu.CompilerParams(dimension_semantics=sem, vmem_limit_bytes=VMEM_LIMIT_BYTES)


def _const_spec(shape):
    nd = len(shape)
    return pl.BlockSpec(shape, lambda *_: (0,) * nd, pipeline_mode=pl.Buffered(1))


def _ada_kernel(c_ref, w_ref, b_ref, o_ref):
    s = jax.nn.silu(c_ref[...]).astype(BF16)
    o_ref[0] = _dot(s, w_ref[0]) + b_ref[0]


def _ada(c_all, w_ada, b_ada):
    depth, d, n = w_ada.shape
    r = c_all.shape[0]
    tn = 2048
    return pl.pallas_call(
        _ada_kernel,
        grid=(depth, n // tn),
        in_specs=[
            pl.BlockSpec((r, d), lambda l, j: (0, 0)),
            pl.BlockSpec((1, d, tn), lambda l, j: (l, 0, j)),
            pl.BlockSpec((1, 1, tn), lambda l, j: (l, 0, j)),
        ],
        out_specs=pl.BlockSpec((1, r, tn), lambda l, j: (l, 0, j)),
        out_shape=jax.ShapeDtypeStruct((depth, r, n), F32),
        compiler_params=_params("parallel", "parallel"),
        name="ada_mod",
    )(c_all, w_ada, b_ada.reshape(depth, 1, n))


def _inproj_kernel(x_ref, sc_ref, sh_ref, w_ref, wif_ref, wift_ref, *refs, chunk):
    q_ref, k_ref, v_ref, mqk_ref, mv_ref, mo_ref, gc_ref, gr_ref = refs[-8:]
    for r in _row_parts(x_ref.shape[1]):
        n_rows = r.stop - r.start
        h = (_layer_norm(x_ref[0, r, :]) * (1.0 + sc_ref[0]) + sh_ref[0]).astype(BF16)

        def proj(lo, hi):
            return _dot(h, w_ref[:, lo:hi])

        q_ref[0, r, :] = (proj(0, 512) * Q_SCALE).astype(BF16)
        for out_ref, lo in ((k_ref, DA_WIDTH), (v_ref, 2 * DA_WIDTH)):
            rows = proj(lo, lo + DA_WIDTH)
            for hd in range(DA_HEADS):
                dst = pl.ds(DA_HEADS * r.start + hd, n_rows, stride=DA_HEADS)
                out_ref[0, 0, dst, :] = rows[:, hd * DA_QK:(hd + 1) * DA_QK]
        mqk_ref[0, r, :] = proj(1536, 2560)
        mv_ref[0, r, :] = proj(2560, 3072).astype(BF16)
        mo_ref[0, r, :] = proj(3072, 3584)
        gc_ref[0, r, :] = _dot(h, wif_ref[...])
        gr = _dot_nt(wift_ref[...], h)
        for j in range(n_rows // chunk):
            gr_ref[0, r.start // chunk + j] = gr[:, j * chunk:(j + 1) * chunk]


def _inproj(x, sc, sh, w_a, w_if, w_ift, kv_prev, *, layer, depth, tm, chunk):
    b, t, d = x.shape
    n_chunks = tm // chunk
    tok = lambda n: pl.BlockSpec((1, tm, n), lambda i, j: (i, j, 0))
    row = pl.BlockSpec((1, 1, d), lambda i, j: (i, 0, 0))
    kv_shape = jax.ShapeDtypeStruct((depth, b, DA_HEADS * t, DA_QK), F32)
    kv_spec = pl.BlockSpec((1, 1, DA_HEADS * tm, DA_QK), lambda i, j: (layer, i, j, 0))
    n_in = 6
    aliased = () if kv_prev is None else tuple(kv_prev)
    out_shape = (
        jax.ShapeDtypeStruct((b, t, DA_WIDTH), BF16),
        kv_shape,
        kv_shape,
        jax.ShapeDtypeStruct((b, t, 2 * M_WIDTH), F32),
        jax.ShapeDtypeStruct((b, t, M_WIDTH), BF16),
        jax.ShapeDtypeStruct((b, t, M_WIDTH), F32),
        jax.ShapeDtypeStruct((b, t, LANES), F32),
        jax.ShapeDtypeStruct((b, t // chunk, N_GATES, chunk), F32),
    )
    out_specs = (
        tok(DA_WIDTH), kv_spec, kv_spec, tok(2 * M_WIDTH), tok(M_WIDTH), tok(M_WIDTH), tok(LANES),
        pl.BlockSpec((1, n_chunks, N_GATES, chunk), lambda i, j: (i, j, 0, 0)),
    )
    return pl.pallas_call(
        functools.partial(_inproj_kernel, chunk=chunk),
        grid=(b, t // tm),
        in_specs=[tok(d), row, row, _const_spec(w_a.shape), _const_spec(w_if.shape), _const_spec(w_ift.shape)]
        + [pl.BlockSpec(memory_space=pl.ANY)] * len(aliased),
        out_specs=out_specs,
        out_shape=out_shape,
        input_output_aliases={n_in + a: 1 + a for a in range(len(aliased))},
        compiler_params=_params("parallel", "parallel"),
        name="in_proj",
    )(x, sc, sh, w_a, w_if, w_ift, *aliased)


def _lambda(lp_ref, lam_init):
    lp = lp_ref[...]
    a = jnp.sum(lp[0:1] * lp[1:2], axis=1, keepdims=True)
    b = jnp.sum(lp[2:3] * lp[3:4], axis=1, keepdims=True)
    return jnp.exp(a) - jnp.exp(b) + lam_init


def _attn_prompt_kernel(q_ref, k_ref, v_ref, lp_ref, g_ref, o_ref,
                        kb, vt, nrel, qs, sc, acc, mpart, lpart, *, tq, t, lam_init):
    qi = pl.program_id(1)
    tk = tq
    coef = [s * LOG2E for s in ALIBI_SLOPES]

    @pl.when(qi == 0)
    def _():
        for c in range(t // tk):
            rows = slice(c * tk, (c + 1) * tk)
            for h in range(DA_HEADS):
                hc = slice(h * DA_QK, (h + 1) * DA_QK)
                src = pl.ds(DA_HEADS * c * tk + h, tk, stride=DA_HEADS)
                kb[rows, hc] = k_ref[0, 0, src, :].astype(BF16)
                vt[hc, rows] = v_ref[0, 0, src, :].T.astype(BF16)
        rel = (lax.broadcasted_iota(jnp.int32, (tk, tq), 1) - lax.broadcasted_iota(jnp.int32, (tk, tq), 0)).astype(F32)
        for h in range(DA_HEADS):
            nrel[h] = -coef[h] * rel

    lane = lax.broadcasted_iota(jnp.int32, (tq, DA_QK), 1)
    zero = jnp.zeros((tq, DA_QK), BF16)
    for h in range(DA_HEADS):
        qh = q_ref[0, :, h * DA_QK:(h + 1) * DA_QK]
        qs[2 * h] = jnp.where(lane < DA_HD, qh, zero)
        qs[2 * h + 1] = jnp.where(lane >= DA_HD, qh, zero)

    mpart[...] = jnp.full(mpart.shape, -jnp.inf, F32)
    groups = tk // SUBLANES

    def shift(kj, h):
        return (-coef[h] * tq) * (qi - kj).astype(F32)

    def scores(kj, masked):
        off = pl.multiple_of(kj * tk, tk)
        if masked:
            krow = lax.broadcasted_iota(jnp.int32, (tk, tq), 0)
            qcol = lax.broadcasted_iota(jnp.int32, (tk, tq), 1)
            visible = (krow // CHUNK) <= (qcol // CHUNK)
        for h in range(DA_HEADS):
            kblk = kb[pl.ds(off, tk), h * DA_QK:(h + 1) * DA_QK]
            bias = nrel[h]
            if masked:
                bias = jnp.minimum(bias, -bias)
            for c in range(2):
                i = 2 * h + c
                s = _dot_nt(kblk, qs[i]) + bias
                if masked:
                    s = jnp.where(visible, s, -jnp.inf)
                sc[i, pl.ds(off, tk), :] = s
                blk_max = jnp.max(s.reshape(groups, SUBLANES, tq), axis=0)
                mpart[i] = jnp.maximum(mpart[i], blk_max if masked else blk_max + shift(kj, h))

    def scores_body(kj, carry):
        scores(kj, masked=False)
        return carry

    lax.fori_loop(0, qi, scores_body, 0)
    scores(qi, masked=True)

    for i in range(2 * DA_HEADS):
        mpart[i] = jnp.broadcast_to(jnp.max(mpart[i], axis=0, keepdims=True), (SUBLANES, tq))
    lpart[...] = jnp.zeros(lpart.shape, F32)
    acc[...] = jnp.zeros(acc.shape, F32)

    def weights_body(kj, carry):
        off = pl.multiple_of(kj * tk, tk)
        for h in range(DA_HEADS):
            vblk = vt[h * DA_VD:(h + 1) * DA_VD, pl.ds(off, tk)]
            for c in range(2):
                i = 2 * h + c
                s = sc[i, pl.ds(off, tk), :].reshape(groups, SUBLANES, tq)
                p = jnp.exp2(s - (mpart[i] - shift(kj, h))[None])
                lpart[i] += jnp.sum(p, axis=0)
                acc[i] += _dot(vblk, p.reshape(tk, tq).astype(BF16))
        return carry

    lax.fori_loop(0, qi + 1, weights_body, 0)

    lam = _lambda(lp_ref, lam_init)
    for h in range(DA_HEADS):
        hc = slice(h * DA_VD, (h + 1) * DA_VD)
        l0 = jnp.sum(lpart[2 * h], axis=0, keepdims=True)
        l1 = jnp.sum(lpart[2 * h + 1], axis=0, keepdims=True)
        o = acc[2 * h] * (1.0 / l0) - acc[2 * h + 1] * (lam / l1)
        o = o * lax.rsqrt(jnp.mean(o * o, axis=0, keepdims=True) + LN_EPS)
        o_ref[0, :, hc] = (o.T * g_ref[:, hc] * (1.0 - lam_init)).astype(BF16)


def _attn_prompt(q, k, v, lam_p, gain, *, layer, lam_init, tq):
    b, t, _ = q.shape
    kv = pl.BlockSpec((1, 1, DA_HEADS * t, DA_QK), lambda i, j: (layer, i, 0, 0))
    tok = pl.BlockSpec((1, tq, DA_WIDTH), lambda i, j: (i, j, 0))
    return pl.pallas_call(
        functools.partial(_attn_prompt_kernel, tq=tq, t=t, lam_init=lam_init),
        grid=(b, t // tq),
        in_specs=[tok, kv, kv, _const_spec(lam_p.shape), _const_spec(gain.shape)],
        out_specs=tok,
        out_shape=jax.ShapeDtypeStruct((b, t, DA_WIDTH), BF16),
        scratch_shapes=[
            pltpu.VMEM((t, DA_WIDTH), BF16),
            pltpu.VMEM((DA_WIDTH, t), BF16),
            pltpu.VMEM((DA_HEADS, tq, tq), F32),
            pltpu.VMEM((2 * DA_HEADS, tq, DA_QK), BF16),
            pltpu.VMEM((2 * DA_HEADS, t, tq), F32),
            pltpu.VMEM((2 * DA_HEADS, DA_VD, tq), F32),
            pltpu.VMEM((2 * DA_HEADS, SUBLANES, tq), F32),
            pltpu.VMEM((2 * DA_HEADS, SUBLANES, tq), F32),
        ],
        compiler_params=_params("parallel", "arbitrary"),
        name="diff_attn_prompt",
    )(q, k, v, lam_p, gain)


def _attn_sample_kernel(q_ref, k_ref, v_ref, ck_ref, cv_ref, lp_ref, g_ref, o_ref, *, past, ts, lam_init):
    lam = _lambda(lp_ref, lam_init)

    def bias_mask(n_keys, key_base):
        qpos = past + lax.broadcasted_iota(jnp.int32, (ts, n_keys), 0)
        kpos = key_base + lax.broadcasted_iota(jnp.int32, (ts, n_keys), 1)
        return jnp.abs(qpos - kpos).astype(F32), (kpos // CHUNK) <= (qpos // CHUNK)

    dist_c, vis_c = bias_mask(past, 0)
    dist_n, vis_n = bias_mask(ts, past)
    lane = lax.broadcasted_iota(jnp.int32, (ts, DA_QK), 1)
    zero = jnp.zeros((ts, DA_QK), BF16)
    for h in range(DA_HEADS):
        hc = slice(h * DA_QK, (h + 1) * DA_QK)
        q = q_ref[0, :, hc]
        qc = (jnp.where(lane < DA_HD, q, zero), jnp.where(lane >= DA_HD, q, zero))
        coef = -LOG2E * ALIBI_SLOPES[h]
        old = pl.ds(h, past, stride=DA_HEADS)
        new = pl.ds(h, ts, stride=DA_HEADS)
        ck = ck_ref[0, 0, old, :].astype(BF16)
        cv = cv_ref[0, 0, old, :].astype(BF16)
        kn = k_ref[0, 0, new, :].astype(BF16)
        vn = v_ref[0, 0, new, :].astype(BF16)
        outs = []
        for c in range(2):
            s_c = jnp.where(vis_c, _dot_nt(qc[c], ck) + coef * dist_c, -jnp.inf)
            s_n = jnp.where(vis_n, _dot_nt(qc[c], kn) + coef * dist_n, -jnp.inf)
            m = jnp.maximum(jnp.max(s_c, axis=1, keepdims=True), jnp.max(s_n, axis=1, keepdims=True))
            p_c = jnp.exp2(s_c - m)
            p_n = jnp.exp2(s_n - m)
            l = jnp.sum(p_c, axis=1, keepdims=True) + jnp.sum(p_n, axis=1, keepdims=True)
            outs.append((_dot(p_c.astype(BF16), cv) + _dot(p_n.astype(BF16), vn)) / l)
        o = outs[0] - lam * outs[1]
        o = o * lax.rsqrt(jnp.mean(o * o, axis=1, keepdims=True) + LN_EPS)
        o_ref[0, :, hc] = (o * g_ref[:, hc] * (1.0 - lam_init)).astype(BF16)


def _attn_sample(q, k, v, cache_k, cache_v, lam_p, gain, *, layer, lam_init):
    b, ts, _ = q.shape
    past = cache_k.shape[2] // DA_HEADS
    tok = pl.BlockSpec((1, ts, DA_WIDTH), lambda i: (i, 0, 0))
    new = pl.BlockSpec((1, 1, DA_HEADS * ts, DA_QK), lambda i: (layer, i, 0, 0))
    old = pl.BlockSpec((1, 1, DA_HEADS * past, DA_QK), lambda i: (layer, i, 0, 0))
    return pl.pallas_call(
        functools.partial(_attn_sample_kernel, past=past, ts=ts, lam_init=lam_init),
        grid=(b,),
        in_specs=[tok, new, new, old, old, _const_spec(lam_p.shape), _const_spec(gain.shape)],
        out_specs=tok,
        out_shape=jax.ShapeDtypeStruct((b, ts, DA_WIDTH), BF16),
        compiler_params=_params("parallel"),
        name="diff_attn_sample",
    )(q, k, v, cache_k, cache_v, lam_p, gain)


def _split3(x):
    hi = x.astype(BF16)
    r = x - hi.astype(F32)
    mid = r.astype(BF16)
    lo = (r - mid.astype(F32)).astype(BF16)
    return hi, mid, lo


def _mlstm_kernel(mqk_ref, mv_ref, mo_ref, gc_ref, gr_ref, cw_ref, cb_ref, bc_ref, br_ref, nw_ref,
                  c0_ref, n0_ref, m0_ref, cv0_ref,
                  o_ref, cf_ref, nf_ref, mf_ref, cvf_ref,
                  ubuf, qk_s, c_s, n_s, m_s, *, chunk, tb):
    t = pl.program_id(1)
    n_t = pl.num_programs(1)
    width = 2 * M_WIDTH

    @pl.when(t == 0)
    def _():
        ubuf[0:SUBLANES, :] = cv0_ref[0]
        c_s[...] = c0_ref[0]
        n_s[...] = n0_ref[0]
        m_s[...] = m0_ref[0]

    ubuf[SUBLANES:SUBLANES + tb, :] = mqk_ref[0]
    k_scale = M_HD ** -0.5
    for j in range(width // LANES):
        cols = slice(j * LANES, (j + 1) * LANES)
        y = cb_ref[:, cols]
        for tap in range(CONV_W):
            start = SUBLANES - (CONV_W - 1) + tap
            y = y + ubuf[start:start + tb, cols] * cw_ref[tap:tap + 1, cols]
        y = jax.nn.silu(y)
        if j >= M_WIDTH // LANES:
            y = y * k_scale
        qk_s[:, cols] = y.astype(BF16)
    ubuf[0:SUBLANES, :] = ubuf[tb:tb + SUBLANES, :]

    row_i = lax.broadcasted_iota(jnp.int32, (chunk, chunk), 0)
    col_i = lax.broadcasted_iota(jnp.int32, (chunk, chunk), 1)
    tril = (col_i <= row_i).astype(BF16)
    triu = (row_i <= col_i).astype(BF16)
    seen = row_i <= col_i
    last = chunk - 1

    for ci in range(tb // chunk):
        rows = slice(ci * chunk, (ci + 1) * chunk)
        gcb = gc_ref[0, rows, :] + bc_ref[...]
        grb = gr_ref[0, ci] + br_ref[...]
        b_cols = sum(_dot(tril, part) for part in _split3(jax.nn.log_sigmoid(gcb)))
        b_rows = sum(_dot(part, triu) for part in _split3(jax.nn.log_sigmoid(grb)))
        g_cols = gcb - pltpu.roll(b_cols, LANES - M_HEADS, 1)
        n_bf = n_s[...].astype(BF16)

        for h in range(M_HEADS):
            hc = slice(h * M_HD, (h + 1) * M_HD)
            q = qk_s[rows, hc]
            k = qk_s[rows, M_WIDTH + h * M_HD:M_WIDTH + (h + 1) * M_HD]
            v = mv_ref[0, rows, hc]
            b_row = b_rows[M_HEADS + h:M_HEADS + h + 1, :]
            m_prev = m_s[h:h + 1, 0:1]
            c_prev = c_s[h]
            n_prev = n_s[h:h + 1, :]

            g_b = jnp.broadcast_to(g_cols[:, h:h + 1], (chunk, LANES))
            g_m = jnp.where(seen, g_b[:, :chunk], -jnp.inf)
            g_max = jnp.max(g_m, axis=0, keepdims=True)
            p_t = _dot_nt(k, q) * jnp.exp(g_m - g_max)
            r = jnp.sum(p_t, axis=0, keepdims=True)
            u_t = _dot_tn(v, p_t.astype(BF16))
            g_last = g_max[:, last:]
            ws_b = jnp.exp(g_b - g_last)
            dc = _dot_tn((v.astype(F32) * ws_b).astype(BF16), k)
            dn = jnp.sum(ws_b * k.astype(F32), axis=0, keepdims=True)

            x = g_max - m_prev
            e1 = jnp.exp(jnp.minimum(x, 0.0))
            e2 = jnp.exp(jnp.minimum(-x, 0.0))
            m_t = b_row + jnp.maximum(m_prev, g_max)
            qc_t = _dot_nt(c_prev.astype(BF16), q)
            qn = _dot_nt(n_bf, q)[h:h + 1, :]
            den = e1 * r + e2 * qn
            h_t = (e1 * u_t + e2 * qc_t) * (1.0 / jnp.maximum(jnp.abs(den), jnp.exp(-m_t)))
            f, wc = e1[:, last:], e2[:, last:]
            c_s[h] = wc * c_prev + f * dc
            n_s[h:h + 1, :] = wc * n_prev + f * dn
            m_s[h:h + 1, :] = jnp.broadcast_to(b_row[:, last:] + jnp.maximum(m_prev, g_last), (1, LANES))

            hcen = h_t - jnp.mean(h_t, axis=0, keepdims=True)
            hn = (hcen * lax.rsqrt(jnp.mean(hcen * hcen, axis=0, keepdims=True) + LN_EPS)).T
            o_ref[0, rows, hc] = (hn * nw_ref[:, hc] * jax.nn.sigmoid(mo_ref[0, rows, hc])).astype(BF16)

    @pl.when(t == n_t - 1)
    def _():
        cf_ref[0] = c_s[...]
        nf_ref[0] = n_s[...]
        mf_ref[0] = m_s[...]
        cvf_ref[0] = ubuf[0:SUBLANES, :]


def _mlstm(mqk, mv, mo, gc, gr, conv_w, conv_b, b_if, norm_w, c0, n0, m0, cv0, *, chunk, tb):
    b, t, width = mqk.shape
    tok = lambda n: pl.BlockSpec((1, tb, n), lambda i, j: (i, j, 0))
    per_b = lambda shape: pl.BlockSpec((1,) + shape, lambda i, j: (i,) + (0,) * len(shape))
    bc = jnp.zeros((1, LANES), F32).at[0, :N_GATES].set(b_if)
    br = b_if.reshape(N_GATES, 1)
    out_shape = (
        jax.ShapeDtypeStruct((b, t, M_WIDTH), BF16),
        jax.ShapeDtypeStruct((b, M_HEADS, M_HD, M_HD), F32),
        jax.ShapeDtypeStruct((b, SUBLANES, M_HD), F32),
        jax.ShapeDtypeStruct((b, SUBLANES, LANES), F32),
        jax.ShapeDtypeStruct((b, SUBLANES, width), F32),
    )
    return pl.pallas_call(
        functools.partial(_mlstm_kernel, chunk=chunk, tb=tb),
        grid=(b, t // tb),
        in_specs=[
            tok(width), tok(M_WIDTH), tok(M_WIDTH), tok(LANES),
            pl.BlockSpec((1, tb // chunk, N_GATES, chunk), lambda i, j: (i, j, 0, 0)),
            _const_spec(conv_w.shape), _const_spec((1, width)), _const_spec(bc.shape), _const_spec(br.shape),
            _const_spec((1, M_WIDTH)),
            per_b((M_HEADS, M_HD, M_HD)), per_b((SUBLANES, M_HD)), per_b((SUBLANES, LANES)), per_b((SUBLANES, width)),
        ],
        out_specs=(tok(M_WIDTH), per_b((M_HEADS, M_HD, M_HD)), per_b((SUBLANES, M_HD)),
                   per_b((SUBLANES, LANES)), per_b((SUBLANES, width))),
        out_shape=out_shape,
        scratch_shapes=[
            pltpu.VMEM((SUBLANES + tb, width), F32),
            pltpu.VMEM((tb, width), BF16),
            pltpu.VMEM((M_HEADS, M_HD, M_HD), F32),
            pltpu.VMEM((SUBLANES, M_HD), F32),
            pltpu.VMEM((SUBLANES, LANES), F32),
        ],
        compiler_params=_params("parallel", "arbitrary"),
        name="mlstm",
    )(mqk, mv, mo, gc, gr, conv_w, conv_b.reshape(1, width), bc, br, norm_w.reshape(1, M_WIDTH), c0, n0, m0, cv0)


def _mix_kernel(x_ref, an_ref, mn_ref, sc_ref, sh_ref, g1_ref, wa_ref, wb_ref, wg_ref, bg_ref, wo_ref,
                lg_ref, lb_ref, o_ref, *, alpha):
    d = x_ref.shape[-1]
    for r in _row_parts(x_ref.shape[1]):
        x = x_ref[0, r, :]
        h = (_layer_norm(x) * (1.0 + sc_ref[0]) + sh_ref[0]).astype(BF16)
        y_a = _dot(an_ref[0, r, :], wa_ref[...])
        y_b = _dot(mn_ref[0, r, :], wb_ref[...])
        g_a = jax.nn.sigmoid(_dot(h, wg_ref[:, :d]) + bg_ref[:, :d])
        g_b = jax.nn.sigmoid(_dot(h, wg_ref[:, d:]) + bg_ref[:, d:])
        mix = _dot((g_a * y_a + g_b * y_b).astype(BF16), wo_ref[...])
        o_ref[0, r, :] = _layer_norm(alpha * x + (1.0 + g1_ref[0]) * mix) * lg_ref[...] + lb_ref[...]


def _mix(x, an, mn, sc, sh, g1, w_a, w_b, w_g, b_g, w_o, ln_g, ln_b, *, tm, alpha):
    b, t, d = x.shape
    tok = lambda n: pl.BlockSpec((1, tm, n), lambda i, j: (i, j, 0))
    row = pl.BlockSpec((1, 1, d), lambda i, j: (i, 0, 0))
    return pl.pallas_call(
        functools.partial(_mix_kernel, alpha=alpha),
        grid=(b, t // tm),
        in_specs=[tok(d), tok(DA_WIDTH), tok(M_WIDTH), row, row, row,
                  _const_spec(w_a.shape), _const_spec(w_b.shape), _const_spec(w_g.shape), _const_spec((1, 2 * d)),
                  _const_spec(w_o.shape), _const_spec((1, d)), _const_spec((1, d))],
        out_specs=tok(d),
        out_shape=jax.ShapeDtypeStruct((b, t, d), F32),
        compiler_params=_params("parallel", "parallel"),
        name="mix_out",
    )(x, an, mn, sc, sh, g1, w_a, w_b, w_g, b_g.reshape(1, 2 * d), w_o, ln_g.reshape(1, d), ln_b.reshape(1, d))


def _ffn_kernel(x_ref, sc_ref, sh_ref, g2_ref, wgu_ref, wd_ref, lg_ref, lb_ref, o_ref, *, alpha, d_ff, fc):
    h = (_layer_norm(x_ref[0]) * (1.0 + sc_ref[0]) + sh_ref[0]).astype(BF16)
    for j in range(d_ff // fc):
        gt = _dot(h, wgu_ref[:, j * fc:(j + 1) * fc])
        up = _dot(h, wgu_ref[:, d_ff + j * fc:d_ff + (j + 1) * fc])
        part = _dot((jax.nn.silu(gt) * up).astype(BF16), wd_ref[j * fc:(j + 1) * fc, :])
        if j == 0:
            o_ref[0] = part
        else:
            o_ref[0] += part
    o_ref[0] = _layer_norm(alpha * x_ref[0] + (1.0 + g2_ref[0]) * o_ref[0]) * lg_ref[...] + lb_ref[...]


def _ffn(x, sc, sh, g2, w_gu, w_down, ln_g, ln_b, *, tm, alpha):
    b, t, d = x.shape
    d_ff = w_down.shape[0]
    fc = 256
    tok = pl.BlockSpec((1, tm, d), lambda i, j: (i, j, 0))
    row = pl.BlockSpec((1, 1, d), lambda i, j: (i, 0, 0))
    return pl.pallas_call(
        functools.partial(_ffn_kernel, alpha=alpha, d_ff=d_ff, fc=fc),
        grid=(b, t // tm),
        in_specs=[tok, row, row, row, _const_spec(w_gu.shape), _const_spec(w_down.shape),
                  _const_spec((1, d)), _const_spec((1, d))],
        out_specs=tok,
        out_shape=jax.ShapeDtypeStruct((b, t, d), F32),
        compiler_params=_params("parallel", "parallel"),
        name="swiglu",
    )(x, sc, sh, g2, w_gu, w_down, ln_g.reshape(1, d), ln_b.reshape(1, d))


def _tile(t, pref):
    return pref if t % pref == 0 else t


def _layer(x, mod, layer, depth, attend, mstate, weights, kv_prev):
    (w_a, w_if, w_ift, b_if, conv_w, conv_b, lam_p, da_norm_w, m_norm_w, w_br_a, w_br_b, w_gate, b_gate, w_o,
     ln1_g, ln1_b, w_gu, w_down, ln2_g, ln2_b) = weights
    b, t, d = x.shape
    alpha = (2 * depth) ** 0.25
    lam_init = 0.8 - 0.6 * math.exp(-0.3 * layer)
    sh1, sc1, g1, sh2, sc2, g2 = (m.reshape(b, 1, d) for m in jnp.split(mod, 6, axis=-1))
    tm = _tile(t, 512)
    chunk = min(CHUNK, t)
    tb = _tile(t, 256)

    q, k, v, mqk, mv, mo, gc, gr = _inproj(x, sc1, sh1, w_a, w_if, w_ift, kv_prev,
                                           layer=layer, depth=depth, tm=tm, chunk=chunk)
    a_n = attend(q, k, v, lam_p, da_norm_w.reshape(1, DA_WIDTH), layer, lam_init)
    m_n, c_f, n_f, m_f, cv_f = _mlstm(mqk, mv, mo, gc, gr, conv_w, conv_b, b_if, m_norm_w, *mstate, chunk=chunk, tb=tb)
    x = _mix(x, a_n, m_n, sc1, sh1, g1, w_br_a, w_br_b, w_gate, b_gate, w_o, ln1_g, ln1_b, tm=tm, alpha=alpha)
    x = _ffn(x, sc2, sh2, g2, w_gu, w_down, ln2_g, ln2_b, tm=tm, alpha=alpha)
    state = (c_f, n_f[:, :M_HEADS], m_f[:, :M_HEADS, 0], cv_f[:, SUBLANES - (CONV_W - 1):])
    return x, (k, v), state


def _pad_rows(a, rows):
    pad = [(0, 0)] * a.ndim
    pad[1] = (rows - a.shape[1], 0)
    return jnp.pad(a, pad)


def kernel(x_prompt, x_sample, c_prompt, c_sample, cache_attn_k, cache_attn_v, state_mlstm_C, state_mlstm_n,
           state_mlstm_m, state_mlstm_conv, w_ada, b_ada, w_in, b_if, conv_w, conv_b, lam_p, da_norm_w, m_norm_w,
           w_br_a, w_br_b, w_gate, b_gate, w_o, ln1_g, ln1_b, w_gu, w_down, ln2_g, ln2_b):
    depth = w_in.shape[0]
    bp, bs = x_prompt.shape[0], x_sample.shape[0]
    past = cache_attn_k.shape[2]

    mod = _ada(jnp.concatenate([c_prompt, c_sample], axis=0), w_ada.astype(BF16), b_ada)

    gate_lo = 3 * DA_WIDTH + 2 * M_WIDTH + M_WIDTH
    gate_hi = gate_lo + N_GATES

    cache_k = cache_attn_k.reshape(depth, bs, past * DA_HEADS, DA_QK)
    cache_v = cache_attn_v.reshape(depth, bs, past * DA_HEADS, DA_VD)

    xp, xs = x_prompt, x_sample
    kv_p = kv_s = None
    outs_p, outs_s = [], []
    for l in range(depth):
        w_l = w_in[l]
        w_a = jnp.concatenate([w_l[:, :gate_lo], w_l[:, gate_hi:]], axis=1).astype(BF16)
        w_if = jnp.pad(w_l[:, gate_lo:gate_hi], ((0, 0), (0, LANES - N_GATES))).astype(BF16)
        w_ift = w_l[:, gate_lo:gate_hi].T.astype(BF16)
        weights = (w_a, w_if, w_ift, b_if[l], conv_w[l], conv_b[l], lam_p[l], da_norm_w[l], m_norm_w[l],
                   w_br_a[l].astype(BF16), w_br_b[l].astype(BF16), w_gate[l].astype(BF16), b_gate[l],
                   w_o[l].astype(BF16), ln1_g[l], ln1_b[l], w_gu[l].astype(BF16), w_down[l].astype(BF16),
                   ln2_g[l], ln2_b[l])

        zero_state = (jnp.zeros((bp, M_HEADS, M_HD, M_HD), F32), jnp.zeros((bp, SUBLANES, M_HD), F32),
                      jnp.zeros((bp, SUBLANES, LANES), F32), jnp.zeros((bp, SUBLANES, 2 * M_WIDTH), F32))
        attend_p = lambda q, k, v, lp, g, layer, lam_init: _attn_prompt(
            q, k, v, lp, g, layer=layer, lam_init=lam_init, tq=_tile(q.shape[1], 256))
        xp, kv_p, st_p = _layer(xp, mod[l, :bp], l, depth, attend_p, zero_state, weights, kv_p)
        outs_p.append(st_p)

        attend_s = lambda q, k, v, lp, g, layer, lam_init: _attn_sample(
            q, k, v, cache_k, cache_v, lp, g, layer=layer, lam_init=lam_init)
        head_pad = ((0, 0), (0, SUBLANES - M_HEADS), (0, 0))
        state_s = (state_mlstm_C[l],
                   jnp.pad(state_mlstm_n[l], head_pad),
                   jnp.pad(jnp.broadcast_to(state_mlstm_m[l][:, :, None], (bs, M_HEADS, LANES)), head_pad),
                   _pad_rows(state_mlstm_conv[l], SUBLANES))
        xs, kv_s, st_s = _layer(xs, mod[l, bp:], l, depth, attend_s, state_s, weights, kv_s)
        outs_s.append(st_s)

    stack = lambda outs, i: jnp.stack([o[i] for o in outs])
    rows_to_heads = lambda a: a.reshape(depth, a.shape[1], a.shape[2] // DA_HEADS, DA_HEADS, DA_QK)
    kp, vp = (rows_to_heads(a) for a in kv_p)
    ks, vs = (rows_to_heads(a) for a in kv_s)
    cp, np_, mp, cvp = (stack(outs_p, i) for i in range(4))
    cs, ns, ms, cvs = (stack(outs_s, i) for i in range(4))
    return (xp, xs, kp, vp, ks, vs, cp, np_, mp, cvp, cs, ns, ms, cvs)
```

```python
import functools
import math

import jax
import jax.numpy as jnp
from jax import lax
from jax.experimental import pallas as pl
from jax.experimental.pallas import tpu as pltpu

F32 = jnp.float32
BF16 = jnp.bfloat16

D_MODEL = 1024
CHUNK = 64
DA_HEADS = 4
DA_HD = 64
DA_QK = 2 * DA_HD
DA_VD = 2 * DA_HD
DA_WIDTH = DA_HEADS * DA_VD
M_HEADS = 4
M_HD = 128
M_WIDTH = M_HEADS * M_HD
CONV_W = 4
N_GATES = 2 * M_HEADS
LN_EPS = 1e-5
LOG2E = math.log2(math.e)
Q_SCALE = DA_HD ** -0.5 * LOG2E
ALIBI_SLOPES = tuple(2.0 ** (-8.0 * (i + 1) / DA_HEADS) for i in range(DA_HEADS))

SUBLANES = 8
LANES = 128
VMEM_LIMIT_BYTES = 56 * 1024 * 1024

_NT = (((1,), (1,)), ((), ()))
_TN = (((0,), (0,)), ((), ()))


def _dot(a, b):
    return jnp.dot(a, b, preferred_element_type=F32)


def _dot_nt(a, b):
    return lax.dot_general(a, b, _NT, preferred_element_type=F32)


def _dot_tn(a, b):
    return lax.dot_general(a, b, _TN, preferred_element_type=F32)


def _layer_norm(x):
    mu = jnp.mean(x, axis=-1, keepdims=True)
    xc = x - mu
    var = jnp.mean(xc * xc, axis=-1, keepdims=True)
    return xc * lax.rsqrt(var + LN_EPS)


ROW_PARTS = 2


def _row_parts(rows):
    if rows % (ROW_PARTS * 128) != 0:
        return (slice(0, rows),)
    part = rows // ROW_PARTS
    return tuple(slice(i * part, (i + 1) * part) for i in range(ROW_PARTS))


def _mod_rows(ref, r):
    return ref[0] if ref.shape[1] == 1 else ref[0, r, :]


def _mod_spec(mod, tm):
    rows = mod.shape[1]
    if rows == 1:
        return pl.BlockSpec((1, 1, mod.shape[2]), lambda i, j: (i, 0, 0))
    return pl.BlockSpec((1, tm, mod.shape[2]), lambda i, j: (i, j, 0))


def _params(*sem):
    return pltpu.CompilerParams(dimension_semantics=sem, vmem_limit_bytes=VMEM_LIMIT_BYTES)


def _const_spec(shape):
    nd = len(shape)
    return pl.BlockSpec(shape, lambda *_: (0,) * nd, pipeline_mode=pl.Buffered(1))


def _ada_kernel(c_ref, w_ref, b_ref, o_ref):
    s = jax.nn.silu(c_ref[...]).astype(BF16)
    o_ref[0] = _dot(s, w_ref[0]) + b_ref[0]


def _ada(c_all, w_ada, b_ada):
    depth, d, n = w_ada.shape
    r = c_all.shape[0]
    tn = 2048
    return pl.pallas_call(
        _ada_kernel,
        grid=(depth, n // tn),
        in_specs=[
            pl.BlockSpec((r, d), lambda l, j: (0, 0)),
            pl.BlockSpec((1, d, tn), lambda l, j: (l, 0, j)),
            pl.BlockSpec((1, 1, tn), lambda l, j: (l, 0, j)),
        ],
        out_specs=pl.BlockSpec((1, r, tn), lambda l, j: (l, 0, j)),
        out_shape=jax.ShapeDtypeStruct((depth, r, n), F32),
        compiler_params=_params("parallel", "parallel"),
        name="ada_mod",
    )(c_all, w_ada, b_ada.reshape(depth, 1, n))


def _inproj_kernel(x_ref, sc_ref, sh_ref, w_ref, wif_ref, wift_ref, *refs, chunk):
    q_ref, k_ref, v_ref, mqk_ref, mv_ref, mo_ref, gc_ref, gr_ref = refs[-8:]
    for r in _row_parts(x_ref.shape[1]):
        n_rows = r.stop - r.start
        h = (_layer_norm(x_ref[0, r, :]) * (1.0 + _mod_rows(sc_ref, r)) + _mod_rows(sh_ref, r)).astype(BF16)

        def proj(lo, hi):
            return _dot(h, w_ref[:, lo:hi])

        q_ref[0, r, :] = (proj(0, 512) * Q_SCALE).astype(BF16)
        for out_ref, lo in ((k_ref, DA_WIDTH), (v_ref, 2 * DA_WIDTH)):
            rows = proj(lo, lo + DA_WIDTH)
            for hd in range(DA_HEADS):
                dst = pl.ds(DA_HEADS * r.start + hd, n_rows, stride=DA_HEADS)
                out_ref[0, 0, dst, :] = rows[:, hd * DA_QK:(hd + 1) * DA_QK]
        mqk_ref[0, r, :] = proj(1536, 2560)
        mv_ref[0, r, :] = proj(2560, 3072).astype(BF16)
        mo_ref[0, r, :] = proj(3072, 3584)
        gc_ref[0, r, :] = _dot(h, wif_ref[...])
        gr = _dot_nt(wift_ref[...], h)
        for j in range(n_rows // chunk):
            gr_ref[0, r.start // chunk + j] = gr[:, j * chunk:(j + 1) * chunk]


def _inproj(x, sc, sh, w_a, w_if, w_ift, kv_prev, *, layer, depth, tm, chunk):
    b, t, d = x.shape
    n_chunks = tm // chunk
    tok = lambda n: pl.BlockSpec((1, tm, n), lambda i, j: (i, j, 0))
    row = _mod_spec(sc, tm)
    kv_shape = jax.ShapeDtypeStruct((depth, b, DA_HEADS * t, DA_QK), F32)
    kv_spec = pl.BlockSpec((1, 1, DA_HEADS * tm, DA_QK), lambda i, j: (layer, i, j, 0))
    n_in = 6
    aliased = () if kv_prev is None else tuple(kv_prev)
    out_shape = (
        jax.ShapeDtypeStruct((b, t, DA_WIDTH), BF16),
        kv_shape,
        kv_shape,
        jax.ShapeDtypeStruct((b, t, 2 * M_WIDTH), F32),
        jax.ShapeDtypeStruct((b, t, M_WIDTH), BF16),
        jax.ShapeDtypeStruct((b, t, M_WIDTH), F32),
        jax.ShapeDtypeStruct((b, t, LANES), F32),
        jax.ShapeDtypeStruct((b, t // chunk, N_GATES, chunk), F32),
    )
    out_specs = (
        tok(DA_WIDTH), kv_spec, kv_spec, tok(2 * M_WIDTH), tok(M_WIDTH), tok(M_WIDTH), tok(LANES),
        pl.BlockSpec((1, n_chunks, N_GATES, chunk), lambda i, j: (i, j, 0, 0)),
    )
    return pl.pallas_call(
        functools.partial(_inproj_kernel, chunk=chunk),
        grid=(b, t // tm),
        in_specs=[tok(d), row, row, _const_spec(w_a.shape), _const_spec(w_if.shape), _const_spec(w_ift.shape)]
        + [pl.BlockSpec(memory_space=pl.ANY)] * len(aliased),
        out_specs=out_specs,
        out_shape=out_shape,
        input_output_aliases={n_in + a: 1 + a for a in range(len(aliased))},
        compiler_params=_params("parallel", "parallel"),
        name="in_proj",
    )(x, sc, sh, w_a, w_if, w_ift, *aliased)


def _lambda(lp_ref, lam_init):
    lp = lp_ref[...]
    a = jnp.sum(lp[0:1] * lp[1:2], axis=1, keepdims=True)
    b = jnp.sum(lp[2:3] * lp[3:4], axis=1, keepdims=True)
    return jnp.exp(a) - jnp.exp(b) + lam_init


def _loop_pairs(n, step):
    def pair(j, carry):
        step(2 * j)
        step(2 * j + 1)
        return carry

    lax.fori_loop(0, lax.shift_right_logical(n, 1), pair, 0)

    @pl.when(lax.bitwise_and(n, 1) == 1)
    def _():
        step(n - 1)


def _attn_prompt_kernel(q_ref, k_ref, v_ref, lp_ref, g_ref, o_ref,
                        kb, vt, nrel, qs, sc, acc, mpart, lpart, *, tq, t, lam_init):
    qi = pl.program_id(1)
    tk = tq
    coef = [s * LOG2E for s in ALIBI_SLOPES]

    @pl.when(qi == 0)
    def _():
        for c in range(t // tk):
            rows = slice(c * tk, (c + 1) * tk)
            for h in range(DA_HEADS):
                hc = slice(h * DA_QK, (h + 1) * DA_QK)
                src = pl.ds(DA_HEADS * c * tk + h, tk, stride=DA_HEADS)
                kb[rows, hc] = k_ref[0, 0, src, :].astype(BF16)
                vt[hc, rows] = v_ref[0, 0, src, :].T.astype(BF16)
        rel = (lax.broadcasted_iota(jnp.int32, (tk, tq), 1) - lax.broadcasted_iota(jnp.int32, (tk, tq), 0)).astype(F32)
        for h in range(DA_HEADS):
            nrel[h] = -coef[h] * rel

    lane = lax.broadcasted_iota(jnp.int32, (tq, DA_QK), 1)
    zero = jnp.zeros((tq, DA_QK), BF16)
    for h in range(DA_HEADS):
        qh = q_ref[0, :, h * DA_QK:(h + 1) * DA_QK]
        qs[2 * h] = jnp.where(lane < DA_HD, qh, zero)
        qs[2 * h + 1] = jnp.where(lane >= DA_HD, qh, zero)

    mpart[...] = jnp.full(mpart.shape, -jnp.inf, F32)
    groups = tk // SUBLANES

    def shift(kj, h):
        return (-coef[h] * tq) * (qi - kj).astype(F32)

    def scores(kj, masked):
        off = pl.multiple_of(kj * tk, tk)
        if masked:
            krow = lax.broadcasted_iota(jnp.int32, (tk, tq), 0)
            qcol = lax.broadcasted_iota(jnp.int32, (tk, tq), 1)
            visible = (krow // CHUNK) <= (qcol // CHUNK)
        for h in range(DA_HEADS):
            kblk = kb[pl.ds(off, tk), h * DA_QK:(h + 1) * DA_QK]
            bias = nrel[h]
            if masked:
                bias = jnp.minimum(bias, -bias)
            for c in range(2):
                i = 2 * h + c
                s = _dot_nt(kblk, qs[i]) + bias
                if masked:
                    s = jnp.where(visible, s, -jnp.inf)
                sc[i, pl.ds(off, tk), :] = s
                blk_max = jnp.max(s.reshape(groups, SUBLANES, tq), axis=0)
                mpart[i] = jnp.maximum(mpart[i], blk_max if masked else blk_max + shift(kj, h))

    _loop_pairs(qi, lambda kj: scores(kj, masked=False))
    scores(qi, masked=True)

    for i in range(2 * DA_HEADS):
        mpart[i] = jnp.broadcast_to(jnp.max(mpart[i], axis=0, keepdims=True), (SUBLANES, tq))
    lpart[...] = jnp.zeros(lpart.shape, F32)
    acc[...] = jnp.zeros(acc.shape, F32)

    def weights(kj):
        off = pl.multiple_of(kj * tk, tk)
        for h in range(DA_HEADS):
            vblk = vt[h * DA_VD:(h + 1) * DA_VD, pl.ds(off, tk)]
            for c in range(2):
                i = 2 * h + c
                s = sc[i, pl.ds(off, tk), :].reshape(groups, SUBLANES, tq)
                p = jnp.exp2(s - (mpart[i] - shift(kj, h))[None])
                lpart[i] += jnp.sum(p, axis=0)
                acc[i] += _dot(vblk, p.reshape(tk, tq).astype(BF16))

    _loop_pairs(qi + 1, weights)

    lam = _lambda(lp_ref, lam_init)
    for h in range(DA_HEADS):
        hc = slice(h * DA_VD, (h + 1) * DA_VD)
        l0 = jnp.sum(lpart[2 * h], axis=0, keepdims=True)
        l1 = jnp.sum(lpart[2 * h + 1], axis=0, keepdims=True)
        o = acc[2 * h] * (1.0 / l0) - acc[2 * h + 1] * (lam / l1)
        o = o * lax.rsqrt(jnp.mean(o * o, axis=0, keepdims=True) + LN_EPS)
        o_ref[0, :, hc] = (o.T * g_ref[:, hc] * (1.0 - lam_init)).astype(BF16)


def _attn_prompt(q, k, v, lam_p, gain, *, layer, lam_init, tq):
    b, t, _ = q.shape
    kv = pl.BlockSpec((1, 1, DA_HEADS * t, DA_QK), lambda i, j: (layer, i, 0, 0))
    tok = pl.BlockSpec((1, tq, DA_WIDTH), lambda i, j: (i, j, 0))
    return pl.pallas_call(
        functools.partial(_attn_prompt_kernel, tq=tq, t=t, lam_init=lam_init),
        grid=(b, t // tq),
        in_specs=[tok, kv, kv, _const_spec(lam_p.shape), _const_spec(gain.shape)],
        out_specs=tok,
        out_shape=jax.ShapeDtypeStruct((b, t, DA_WIDTH), BF16),
        scratch_shapes=[
            pltpu.VMEM((t, DA_WIDTH), BF16),
            pltpu.VMEM((DA_WIDTH, t), BF16),
            pltpu.VMEM((DA_HEADS, tq, tq), F32),
            pltpu.VMEM((2 * DA_HEADS, tq, DA_QK), BF16),
            pltpu.VMEM((2 * DA_HEADS, t, tq), F32),
            pltpu.VMEM((2 * DA_HEADS, DA_VD, tq), F32),
            pltpu.VMEM((2 * DA_HEADS, SUBLANES, tq), F32),
            pltpu.VMEM((2 * DA_HEADS, SUBLANES, tq), F32),
        ],
        compiler_params=_params("parallel", "arbitrary"),
        name="diff_attn_prompt",
    )(q, k, v, lam_p, gain)


def _attn_sample_kernel(q_ref, k_ref, v_ref, ck_ref, cv_ref, lp_ref, g_ref, o_ref, *, past, ts, lam_init):
    lam = _lambda(lp_ref, lam_init)

    def bias_mask(n_keys, key_base):
        qpos = past + lax.broadcasted_iota(jnp.int32, (ts, n_keys), 0)
        kpos = key_base + lax.broadcasted_iota(jnp.int32, (ts, n_keys), 1)
        return jnp.abs(qpos - kpos).astype(F32), (kpos // CHUNK) <= (qpos // CHUNK)

    dist_c, vis_c = bias_mask(past, 0)
    dist_n, vis_n = bias_mask(ts, past)
    lane = lax.broadcasted_iota(jnp.int32, (ts, DA_QK), 1)
    zero = jnp.zeros((ts, DA_QK), BF16)
    for h in range(DA_HEADS):
        hc = slice(h * DA_QK, (h + 1) * DA_QK)
        q = q_ref[0, :, hc]
        qc = (jnp.where(lane < DA_HD, q, zero), jnp.where(lane >= DA_HD, q, zero))
        coef = -LOG2E * ALIBI_SLOPES[h]
        old = pl.ds(h, past, stride=DA_HEADS)
        new = pl.ds(h, ts, stride=DA_HEADS)
        ck = ck_ref[0, 0, old, :].astype(BF16)
        cv = cv_ref[0, 0, old, :].astype(BF16)
        kn = k_ref[0, 0, new, :].astype(BF16)
        vn = v_ref[0, 0, new, :].astype(BF16)
        outs = []
        for c in range(2):
            s_c = jnp.where(vis_c, _dot_nt(qc[c], ck) + coef * dist_c, -jnp.inf)
            s_n = jnp.where(vis_n, _dot_nt(qc[c], kn) + coef * dist_n, -jnp.inf)
            m = jnp.maximum(jnp.max(s_c, axis=1, keepdims=True), jnp.max(s_n, axis=1, keepdims=True))
            p_c = jnp.exp2(s_c - m)
            p_n = jnp.exp2(s_n - m)
            l = jnp.sum(p_c, axis=1, keepdims=True) + jnp.sum(p_n, axis=1, keepdims=True)
            outs.append((_dot(p_c.astype(BF16), cv) + _dot(p_n.astype(BF16), vn)) / l)
        o = outs[0] - lam * outs[1]
        o = o * lax.rsqrt(jnp.mean(o * o, axis=1, keepdims=True) + LN_EPS)
        o_ref[0, :, hc] = (o * g_ref[:, hc] * (1.0 - lam_init)).astype(BF16)


def _attn_sample(q, k, v, cache_k, cache_v, lam_p, gain, *, layer, lam_init):
    b, ts, _ = q.shape
    past = cache_k.shape[2] // DA_HEADS
    tok = pl.BlockSpec((1, ts, DA_WIDTH), lambda i: (i, 0, 0))
    new = pl.BlockSpec((1, 1, DA_HEADS * ts, DA_QK), lambda i: (layer, i, 0, 0))
    old = pl.BlockSpec((1, 1, DA_HEADS * past, DA_QK), lambda i: (layer, i, 0, 0))
    return pl.pallas_call(
        functools.partial(_attn_sample_kernel, past=past, ts=ts, lam_init=lam_init),
        grid=(b,),
        in_specs=[tok, new, new, old, old, _const_spec(lam_p.shape), _const_spec(gain.shape)],
        out_specs=tok,
        out_shape=jax.ShapeDtypeStruct((b, ts, DA_WIDTH), BF16),
        compiler_params=_params("parallel"),
        name="diff_attn_sample",
    )(q, k, v, cache_k, cache_v, lam_p, gain)


def _split3(x):
    hi = x.astype(BF16)
    r = x - hi.astype(F32)
    mid = r.astype(BF16)
    lo = (r - mid.astype(F32)).astype(BF16)
    return hi, mid, lo


def _mlstm_kernel(mqk_ref, mv_ref, mo_ref, gc_ref, gr_ref, cw_ref, cb_ref, bc_ref, br_ref, nw_ref,
                  c0_ref, n0_ref, m0_ref, cv0_ref,
                  o_ref, cf_ref, nf_ref, mf_ref, cvf_ref,
                  ubuf, qk_s, c_s, n_s, m_s, *, chunk, tb):
    t = pl.program_id(1)
    n_t = pl.num_programs(1)
    width = 2 * M_WIDTH

    @pl.when(t == 0)
    def _():
        ubuf[0:SUBLANES, :] = cv0_ref[0]
        c_s[...] = c0_ref[0]
        n_s[...] = n0_ref[0]
        m_s[...] = m0_ref[0]

    ubuf[SUBLANES:SUBLANES + tb, :] = mqk_ref[0]
    k_scale = M_HD ** -0.5
    for j in range(width // LANES):
        cols = slice(j * LANES, (j + 1) * LANES)
        y = cb_ref[:, cols]
        for tap in range(CONV_W):
            start = SUBLANES - (CONV_W - 1) + tap
            y = y + ubuf[start:start + tb, cols] * cw_ref[tap:tap + 1, cols]
        y = jax.nn.silu(y)
        if j >= M_WIDTH // LANES:
            y = y * k_scale
        qk_s[:, cols] = y.astype(BF16)
    ubuf[0:SUBLANES, :] = ubuf[tb:tb + SUBLANES, :]

    row_i = lax.broadcasted_iota(jnp.int32, (chunk, chunk), 0)
    col_i = lax.broadcasted_iota(jnp.int32, (chunk, chunk), 1)
    tril = (col_i <= row_i).astype(BF16)
    triu = (row_i <= col_i).astype(BF16)
    seen = row_i <= col_i
    last = chunk - 1

    for ci in range(tb // chunk):
        rows = slice(ci * chunk, (ci + 1) * chunk)
        gcb = gc_ref[0, rows, :] + bc_ref[...]
        grb = gr_ref[0, ci] + br_ref[...]
        b_cols = sum(_dot(tril, part) for part in _split3(jax.nn.log_sigmoid(gcb)))
        b_rows = sum(_dot(part, triu) for part in _split3(jax.nn.log_sigmoid(grb)))
        g_cols = gcb - pltpu.roll(b_cols, LANES - M_HEADS, 1)
        n_bf = n_s[...].astype(BF16)

        for h in range(M_HEADS):
            hc = slice(h * M_HD, (h + 1) * M_HD)
            q = qk_s[rows, hc]
            k = qk_s[rows, M_WIDTH + h * M_HD:M_WIDTH + (h + 1) * M_HD]
            v = mv_ref[0, rows, hc]
            b_row = b_rows[M_HEADS + h:M_HEADS + h + 1, :]
            m_prev = m_s[h:h + 1, 0:1]
            c_prev = c_s[h]
            n_prev = n_s[h:h + 1, :]

            g_b = jnp.broadcast_to(g_cols[:, h:h + 1], (chunk, LANES))
            g_m = jnp.where(seen, g_b[:, :chunk], -jnp.inf)
            g_max = jnp.max(g_m, axis=0, keepdims=True)
            p_t = _dot_nt(k, q) * jnp.exp(g_m - g_max)
            r = jnp.sum(p_t, axis=0, keepdims=True)
            u_t = _dot_tn(v, p_t.astype(BF16))
            g_last = g_max[:, last:]
            ws_b = jnp.exp(g_b - g_last)
            dc = _dot_tn((v.astype(F32) * ws_b).astype(BF16), k)
            dn = jnp.sum(ws_b * k.astype(F32), axis=0, keepdims=True)

            x = g_max - m_prev
            e1 = jnp.exp(jnp.minimum(x, 0.0))
            e2 = jnp.exp(jnp.minimum(-x, 0.0))
            m_t = b_row + jnp.maximum(m_prev, g_max)
            qc_t = _dot_nt(c_prev.astype(BF16), q)
            qn = _dot_nt(n_bf, q)[h:h + 1, :]
            den = e1 * r + e2 * qn
            h_t = (e1 * u_t + e2 * qc_t) * (1.0 / jnp.maximum(jnp.abs(den), jnp.exp(-m_t)))
            f, wc = e1[:, last:], e2[:, last:]
            c_s[h] = wc * c_prev + f * dc
            n_s[h:h + 1, :] = wc * n_prev + f * dn
            m_s[h:h + 1, :] = jnp.broadcast_to(b_row[:, last:] + jnp.maximum(m_prev, g_last), (1, LANES))

            hcen = h_t - jnp.mean(h_t, axis=0, keepdims=True)
            hn = (hcen * lax.rsqrt(jnp.mean(hcen * hcen, axis=0, keepdims=True) + LN_EPS)).T
            o_ref[0, rows, hc] = (hn * nw_ref[:, hc] * jax.nn.sigmoid(mo_ref[0, rows, hc])).astype(BF16)

    @pl.when(t == n_t - 1)
    def _():
        cf_ref[0] = c_s[...]
        nf_ref[0] = n_s[...]
        mf_ref[0] = m_s[...]
        cvf_ref[0] = ubuf[0:SUBLANES, :]


def _mlstm(mqk, mv, mo, gc, gr, conv_w, conv_b, b_if, norm_w, c0, n0, m0, cv0, *, chunk, tb):
    b, t, width = mqk.shape
    tok = lambda n: pl.BlockSpec((1, tb, n), lambda i, j: (i, j, 0))
    per_b = lambda shape: pl.BlockSpec((1,) + shape, lambda i, j: (i,) + (0,) * len(shape))
    bc = jnp.zeros((1, LANES), F32).at[0, :N_GATES].set(b_if)
    br = b_if.reshape(N_GATES, 1)
    out_shape = (
        jax.ShapeDtypeStruct((b, t, M_WIDTH), BF16),
        jax.ShapeDtypeStruct((b, M_HEADS, M_HD, M_HD), F32),
        jax.ShapeDtypeStruct((b, SUBLANES, M_HD), F32),
        jax.ShapeDtypeStruct((b, SUBLANES, LANES), F32),
        jax.ShapeDtypeStruct((b, SUBLANES, width), F32),
    )
    return pl.pallas_call(
        functools.partial(_mlstm_kernel, chunk=chunk, tb=tb),
        grid=(b, t // tb),
        in_specs=[
            tok(width), tok(M_WIDTH), tok(M_WIDTH), tok(LANES),
            pl.BlockSpec((1, tb // chunk, N_GATES, chunk), lambda i, j: (i, j, 0, 0)),
            _const_spec(conv_w.shape), _const_spec((1, width)), _const_spec(bc.shape), _const_spec(br.shape),
            _const_spec((1, M_WIDTH)),
            per_b((M_HEADS, M_HD, M_HD)), per_b((SUBLANES, M_HD)), per_b((SUBLANES, LANES)), per_b((SUBLANES, width)),
        ],
        out_specs=(tok(M_WIDTH), per_b((M_HEADS, M_HD, M_HD)), per_b((SUBLANES, M_HD)),
                   per_b((SUBLANES, LANES)), per_b((SUBLANES, width))),
        out_shape=out_shape,
        scratch_shapes=[
            pltpu.VMEM((SUBLANES + tb, width), F32),
            pltpu.VMEM((tb, width), BF16),
            pltpu.VMEM((M_HEADS, M_HD, M_HD), F32),
            pltpu.VMEM((SUBLANES, M_HD), F32),
            pltpu.VMEM((SUBLANES, LANES), F32),
        ],
        compiler_params=_params("parallel", "arbitrary"),
        name="mlstm",
    )(mqk, mv, mo, gc, gr, conv_w, conv_b.reshape(1, width), bc, br, norm_w.reshape(1, M_WIDTH), c0, n0, m0, cv0)


def _mix_kernel(x_ref, an_ref, mn_ref, sc_ref, sh_ref, g1_ref, wa_ref, wb_ref, wg_ref, bg_ref, wo_ref,
                lg_ref, lb_ref, o_ref, *, alpha):
    d = x_ref.shape[-1]
    for r in _row_parts(x_ref.shape[1]):
        x = x_ref[0, r, :]
        h = (_layer_norm(x) * (1.0 + _mod_rows(sc_ref, r)) + _mod_rows(sh_ref, r)).astype(BF16)
        y_a = _dot(an_ref[0, r, :], wa_ref[...])
        y_b = _dot(mn_ref[0, r, :], wb_ref[...])
        g_a = jax.nn.sigmoid(_dot(h, wg_ref[:, :d]) + bg_ref[:, :d])
        g_b = jax.nn.sigmoid(_dot(h, wg_ref[:, d:]) + bg_ref[:, d:])
        mix = _dot((g_a * y_a + g_b * y_b).astype(BF16), wo_ref[...])
        o_ref[0, r, :] = _layer_norm(alpha * x + (1.0 + _mod_rows(g1_ref, r)) * mix) * lg_ref[...] + lb_ref[...]


def _mix(x, an, mn, sc, sh, g1, w_a, w_b, w_g, b_g, w_o, ln_g, ln_b, *, tm, alpha):
    b, t, d = x.shape
    tok = lambda n: pl.BlockSpec((1, tm, n), lambda i, j: (i, j, 0))
    row = _mod_spec(sc, tm)
    return pl.pallas_call(
        functools.partial(_mix_kernel, alpha=alpha),
        grid=(b, t // tm),
        in_specs=[tok(d), tok(DA_WIDTH), tok(M_WIDTH), row, row, row,
                  _const_spec(w_a.shape), _const_spec(w_b.shape), _const_spec(w_g.shape), _const_spec((1, 2 * d)),
                  _const_spec(w_o.shape), _const_spec((1, d)), _const_spec((1, d))],
        out_specs=tok(d),
        out_shape=jax.ShapeDtypeStruct((b, t, d), F32),
        compiler_params=_params("parallel", "parallel"),
        name="mix_out",
    )(x, an, mn, sc, sh, g1, w_a, w_b, w_g, b_g.reshape(1, 2 * d), w_o, ln_g.reshape(1, d), ln_b.reshape(1, d))


def _ffn_kernel(x_ref, sc_ref, sh_ref, g2_ref, wgu_ref, wd_ref, lg_ref, lb_ref, o_ref, *, alpha, d_ff, fc):
    h = (_layer_norm(x_ref[0]) * (1.0 + sc_ref[0]) + sh_ref[0]).astype(BF16)
    for j in range(d_ff // fc):
        gt = _dot(h, wgu_ref[:, j * fc:(j + 1) * fc])
        up = _dot(h, wgu_ref[:, d_ff + j * fc:d_ff + (j + 1) * fc])
        part = _dot((jax.nn.silu(gt) * up).astype(BF16), wd_ref[j * fc:(j + 1) * fc, :])
        if j == 0:
            o_ref[0] = part
        else:
            o_ref[0] += part
    o_ref[0] = _layer_norm(alpha * x_ref[0] + (1.0 + g2_ref[0]) * o_ref[0]) * lg_ref[...] + lb_ref[...]


def _ffn(x, sc, sh, g2, w_gu, w_down, ln_g, ln_b, *, tm, alpha):
    b, t, d = x.shape
    d_ff = w_down.shape[0]
    fc = 256
    tok = pl.BlockSpec((1, tm, d), lambda i, j: (i, j, 0))
    row = _mod_spec(sc, tm)
    return pl.pallas_call(
        functools.partial(_ffn_kernel, alpha=alpha, d_ff=d_ff, fc=fc),
        grid=(b, t // tm),
        in_specs=[tok, row, row, row, _const_spec(w_gu.shape), _const_spec(w_down.shape),
                  _const_spec((1, d)), _const_spec((1, d))],
        out_specs=tok,
        out_shape=jax.ShapeDtypeStruct((b, t, d), F32),
        compiler_params=_params("parallel", "parallel"),
        name="swiglu",
    )(x, sc, sh, g2, w_gu, w_down, ln_g.reshape(1, d), ln_b.reshape(1, d))


def _tile(t, pref):
    return pref if t % pref == 0 else t


def _layer(x, mod, layer, depth, attend, mstate, weights, kv_prev):
    (w_a, w_if, w_ift, b_if, conv_w, conv_b, lam_p, da_norm_w, m_norm_w, w_br_a, w_br_b, w_gate, b_gate, w_o,
     ln1_g, ln1_b, w_gu, w_down, ln2_g, ln2_b) = weights
    b, t, d = x.shape
    alpha = (2 * depth) ** 0.25
    lam_init = 0.8 - 0.6 * math.exp(-0.3 * layer)
    mods = [m.reshape(b, 1, d) for m in jnp.split(mod, 6, axis=-1)]
    chunk = min(CHUNK, t)
    tb = _tile(t, 512)
    flat = t % LANES != 0
    fb, ft = (1, b * t) if flat else (b, t)
    if flat:
        mods = [jnp.broadcast_to(m, (b, t, d)).reshape(fb, ft, d) for m in mods]
    sh1, sc1, g1, sh2, sc2, g2 = mods
    tm = _tile(ft, 512)
    tokens = lambda a: a.reshape(fb, ft, a.shape[-1])
    sequences = lambda a: a.reshape(b, t, a.shape[-1])

    q, k, v, mqk, mv, mo, gc, gr = _inproj(tokens(x), sc1, sh1, w_a, w_if, w_ift, kv_prev,
                                           layer=layer, depth=depth, tm=tm, chunk=chunk)
    k_seq, v_seq = (a.reshape(depth, b, DA_HEADS * t, DA_QK) for a in (k, v))
    a_n = attend(sequences(q), k_seq, v_seq, lam_p, da_norm_w.reshape(1, DA_WIDTH), layer, lam_init)
    m_n, c_f, n_f, m_f, cv_f = _mlstm(sequences(mqk), sequences(mv), sequences(mo), sequences(gc),
                                      gr.reshape(b, t // chunk, N_GATES, chunk), conv_w, conv_b, b_if, m_norm_w,
                                      *mstate, chunk=chunk, tb=tb)
    y = _mix(tokens(x), tokens(a_n), tokens(m_n), sc1, sh1, g1, w_br_a, w_br_b, w_gate, b_gate, w_o, ln1_g, ln1_b,
             tm=tm, alpha=alpha)
    y = _ffn(y, sc2, sh2, g2, w_gu, w_down, ln2_g, ln2_b, tm=tm, alpha=alpha)
    state = (c_f, n_f[:, :M_HEADS], m_f[:, :M_HEADS, 0], cv_f[:, SUBLANES - (CONV_W - 1):])
    return sequences(y), (k, v), state


def _pad_rows(a, rows):
    pad = [(0, 0)] * a.ndim
    pad[1] = (rows - a.shape[1], 0)
    return jnp.pad(a, pad)


def kernel(x_prompt, x_sample, c_prompt, c_sample, cache_attn_k, cache_attn_v, state_mlstm_C, state_mlstm_n,
           state_mlstm_m, state_mlstm_conv, w_ada, b_ada, w_in, b_if, conv_w, conv_b, lam_p, da_norm_w, m_norm_w,
           w_br_a, w_br_b, w_gate, b_gate, w_o, ln1_g, ln1_b, w_gu, w_down, ln2_g, ln2_b):
    depth = w_in.shape[0]
    bp, bs = x_prompt.shape[0], x_sample.shape[0]
    past = cache_attn_k.shape[2]

    mod = _ada(jnp.concatenate([c_prompt, c_sample], axis=0), w_ada.astype(BF16), b_ada)

    gate_lo = 3 * DA_WIDTH + 2 * M_WIDTH + M_WIDTH
    gate_hi = gate_lo + N_GATES

    cache_k = cache_attn_k.reshape(depth, bs, past * DA_HEADS, DA_QK)
    cache_v = cache_attn_v.reshape(depth, bs, past * DA_HEADS, DA_VD)

    xp, xs = x_prompt, x_sample
    kv_p = kv_s = None
    outs_p, outs_s = [], []
    for l in range(depth):
        w_l = w_in[l]
        w_a = jnp.concatenate([w_l[:, :gate_lo], w_l[:, gate_hi:]], axis=1).astype(BF16)
        w_if = jnp.pad(w_l[:, gate_lo:gate_hi], ((0, 0), (0, LANES - N_GATES))).astype(BF16)
        w_ift = w_l[:, gate_lo:gate_hi].T.astype(BF16)
        weights = (w_a, w_if, w_ift, b_if[l], conv_w[l], conv_b[l], lam_p[l], da_norm_w[l], m_norm_w[l],
                   w_br_a[l].astype(BF16), w_br_b[l].astype(BF16), w_gate[l].astype(BF16), b_gate[l],
                   w_o[l].astype(BF16), ln1_g[l], ln1_b[l], w_gu[l].astype(BF16), w_down[l].astype(BF16),
                   ln2_g[l], ln2_b[l])

        zero_state = (jnp.zeros((bp, M_HEADS, M_HD, M_HD), F32), jnp.zeros((bp, SUBLANES, M_HD), F32),
                      jnp.zeros((bp, SUBLANES, LANES), F32), jnp.zeros((bp, SUBLANES, 2 * M_WIDTH), F32))
        attend_p = lambda q, k, v, lp, g, layer, lam_init: _attn_prompt(
            q, k, v, lp, g, layer=layer, lam_init=lam_init, tq=_tile(q.shape[1], 256))
        xp, kv_p, st_p = _layer(xp, mod[l, :bp], l, depth, attend_p, zero_state, weights, kv_p)
        outs_p.append(st_p)

        attend_s = lambda q, k, v, lp, g, layer, lam_init: _attn_sample(
            q, k, v, cache_k, cache_v, lp, g, layer=layer, lam_init=lam_init)
        head_pad = ((0, 0), (0, SUBLANES - M_HEADS), (0, 0))
        state_s = (state_mlstm_C[l],
                   jnp.pad(state_mlstm_n[l], head_pad),
                   jnp.pad(jnp.broadcast_to(state_mlstm_m[l][:, :, None], (bs, M_HEADS, LANES)), head_pad),
                   _pad_rows(state_mlstm_conv[l], SUBLANES))
        xs, kv_s, st_s = _layer(xs, mod[l, bp:], l, depth, attend_s, state_s, weights, kv_s)
        outs_s.append(st_s)

    stack = lambda outs, i: jnp.stack([o[i] for o in outs])
    kp, vp = (a.reshape(depth, bp, x_prompt.shape[1], DA_HEADS, DA_QK) for a in kv_p)
    ks, vs = (a.reshape(depth, bs, x_sample.shape[1], DA_HEADS, DA_QK) for a in kv_s)
    cp, np_, mp, cvp = (stack(outs_p, i) for i in range(4))
    cs, ns, ms, cvs = (stack(outs_s, i) for i in range(4))
    return (xp, xs, kp, vp, ks, vs, cp, np_, mp, cvp, cs, ns, ms, cvs)
```

```python
import functools
import math

import jax
import jax.numpy as jnp
from jax import lax
from jax.experimental import pallas as pl
from jax.experimental.pallas import tpu as pltpu

F32 = jnp.float32
BF16 = jnp.bfloat16

D_MODEL = 1024
CHUNK = 64
DA_HEADS = 4
DA_HD = 64
DA_QK = 2 * DA_HD
DA_VD = 2 * DA_HD
DA_WIDTH = DA_HEADS * DA_VD
M_HEADS = 4
M_HD = 128
M_WIDTH = M_HEADS * M_HD
CONV_W = 4
N_GATES = 2 * M_HEADS
LN_EPS = 1e-5
LOG2E = math.log2(math.e)
Q_SCALE = DA_HD ** -0.5 * LOG2E
ALIBI_SLOPES = tuple(2.0 ** (-8.0 * (i + 1) / DA_HEADS) for i in range(DA_HEADS))

SUBLANES = 8
LANES = 128
VMEM_LIMIT_BYTES = 56 * 1024 * 1024

_NT = (((1,), (1,)), ((), ()))
_TN = (((0,), (0,)), ((), ()))


def _dot(a, b):
    return jnp.dot(a, b, preferred_element_type=F32)


def _dot_nt(a, b):
    return lax.dot_general(a, b, _NT, preferred_element_type=F32)


def _dot_tn(a, b):
    return lax.dot_general(a, b, _TN, preferred_element_type=F32)


def _layer_norm(x):
    mu = jnp.mean(x, axis=-1, keepdims=True)
    xc = x - mu
    var = jnp.mean(xc * xc, axis=-1, keepdims=True)
    return xc * lax.rsqrt(var + LN_EPS)


ROW_PARTS = 2
FFN_PART_ROWS = 512


def _row_parts(rows):
    if rows % (ROW_PARTS * 128) != 0:
        return (slice(0, rows),)
    part = rows // ROW_PARTS
    return tuple(slice(i * part, (i + 1) * part) for i in range(ROW_PARTS))


def _mod_rows(ref, r):
    return ref[0] if ref.shape[1] == 1 else ref[0, r, :]


def _mod_spec(mod, tm):
    rows = mod.shape[1]
    if rows == 1:
        return pl.BlockSpec((1, 1, mod.shape[2]), lambda i, j: (i, 0, 0))
    return pl.BlockSpec((1, tm, mod.shape[2]), lambda i, j: (i, j, 0))


def _params(*sem):
    return pltpu.CompilerParams(dimension_semantics=sem, vmem_limit_bytes=VMEM_LIMIT_BYTES)


def _const_spec(shape):
    nd = len(shape)
    return pl.BlockSpec(shape, lambda *_: (0,) * nd, pipeline_mode=pl.Buffered(1))


def _ada_kernel(c_ref, w_ref, b_ref, o_ref):
    s = jax.nn.silu(c_ref[...]).astype(BF16)
    o_ref[0] = _dot(s, w_ref[0]) + b_ref[0]


def _ada(c_all, w_ada, b_ada):
    depth, d, n = w_ada.shape
    r = c_all.shape[0]
    tn = 2048
    return pl.pallas_call(
        _ada_kernel,
        grid=(depth, n // tn),
        in_specs=[
            pl.BlockSpec((r, d), lambda l, j: (0, 0)),
            pl.BlockSpec((1, d, tn), lambda l, j: (l, 0, j)),
            pl.BlockSpec((1, 1, tn), lambda l, j: (l, 0, j)),
        ],
        out_specs=pl.BlockSpec((1, r, tn), lambda l, j: (l, 0, j)),
        out_shape=jax.ShapeDtypeStruct((depth, r, n), F32),
        compiler_params=_params("parallel", "parallel"),
        name="ada_mod",
    )(c_all, w_ada, b_ada.reshape(depth, 1, n))


def _inproj_kernel(x_ref, sc_ref, sh_ref, w_ref, wif_ref, wift_ref, *refs, chunk):
    q_ref, k_ref, v_ref, mqk_ref, mv_ref, mo_ref, gc_ref, gr_ref = refs[-8:]
    for r in _row_parts(x_ref.shape[1]):
        n_rows = r.stop - r.start
        h = (_layer_norm(x_ref[0, r, :]) * (1.0 + _mod_rows(sc_ref, r)) + _mod_rows(sh_ref, r)).astype(BF16)

        def proj(lo, hi):
            return _dot(h, w_ref[:, lo:hi])

        q_ref[0, r, :] = (proj(0, 512) * Q_SCALE).astype(BF16)
        for out_ref, lo in ((k_ref, DA_WIDTH), (v_ref, 2 * DA_WIDTH)):
            rows = proj(lo, lo + DA_WIDTH)
            for hd in range(DA_HEADS):
                dst = pl.ds(DA_HEADS * r.start + hd, n_rows, stride=DA_HEADS)
                out_ref[0, 0, dst, :] = rows[:, hd * DA_QK:(hd + 1) * DA_QK]
        mqk_ref[0, r, :] = proj(1536, 2560)
        mv_ref[0, r, :] = proj(2560, 3072).astype(BF16)
        mo_ref[0, r, :] = proj(3072, 3584)
        gc_ref[0, r, :] = _dot(h, wif_ref[...])
        gr = _dot_nt(wift_ref[...], h)
        for j in range(n_rows // chunk):
            gr_ref[0, r.start // chunk + j] = gr[:, j * chunk:(j + 1) * chunk]


def _inproj(x, sc, sh, w_a, w_if, w_ift, kv_prev, *, layer, depth, tm, chunk):
    b, t, d = x.shape
    n_chunks = tm // chunk
    tok = lambda n: pl.BlockSpec((1, tm, n), lambda i, j: (i, j, 0))
    row = _mod_spec(sc, tm)
    kv_shape = jax.ShapeDtypeStruct((depth, b, DA_HEADS * t, DA_QK), F32)
    kv_spec = pl.BlockSpec((1, 1, DA_HEADS * tm, DA_QK), lambda i, j: (layer, i, j, 0))
    n_in = 6
    aliased = () if kv_prev is None else tuple(kv_prev)
    out_shape = (
        jax.ShapeDtypeStruct((b, t, DA_WIDTH), BF16),
        kv_shape,
        kv_shape,
        jax.ShapeDtypeStruct((b, t, 2 * M_WIDTH), F32),
        jax.ShapeDtypeStruct((b, t, M_WIDTH), BF16),
        jax.ShapeDtypeStruct((b, t, M_WIDTH), F32),
        jax.ShapeDtypeStruct((b, t, LANES), F32),
        jax.ShapeDtypeStruct((b, t // chunk, N_GATES, chunk), F32),
    )
    out_specs = (
        tok(DA_WIDTH), kv_spec, kv_spec, tok(2 * M_WIDTH), tok(M_WIDTH), tok(M_WIDTH), tok(LANES),
        pl.BlockSpec((1, n_chunks, N_GATES, chunk), lambda i, j: (i, j, 0, 0)),
    )
    return pl.pallas_call(
        functools.partial(_inproj_kernel, chunk=chunk),
        grid=(b, t // tm),
        in_specs=[tok(d), row, row, _const_spec(w_a.shape), _const_spec(w_if.shape), _const_spec(w_ift.shape)]
        + [pl.BlockSpec(memory_space=pl.ANY)] * len(aliased),
        out_specs=out_specs,
        out_shape=out_shape,
        input_output_aliases={n_in + a: 1 + a for a in range(len(aliased))},
        compiler_params=_params("parallel", "parallel"),
        name="in_proj",
    )(x, sc, sh, w_a, w_if, w_ift, *aliased)


def _lambda(lp_ref, lam_init):
    lp = lp_ref[...]
    a = jnp.sum(lp[0:1] * lp[1:2], axis=1, keepdims=True)
    b = jnp.sum(lp[2:3] * lp[3:4], axis=1, keepdims=True)
    return jnp.exp(a) - jnp.exp(b) + lam_init


def _loop_pairs(n, step):
    def pair(j, carry):
        step(2 * j)
        step(2 * j + 1)
        return carry

    lax.fori_loop(0, lax.shift_right_logical(n, 1), pair, 0)

    @pl.when(lax.bitwise_and(n, 1) == 1)
    def _():
        step(n - 1)


def _attn_prompt_kernel(q_ref, k_ref, v_ref, lp_ref, g_ref, o_ref,
                        kb, vt, nrel, qs, sc, acc, mpart, lpart, *, tq, t, lam_init):
    qi = pl.program_id(1)
    tk = tq
    coef = [s * LOG2E for s in ALIBI_SLOPES]

    @pl.when(qi == 0)
    def _():
        for c in range(t // tk):
            rows = slice(c * tk, (c + 1) * tk)
            for h in range(DA_HEADS):
                hc = slice(h * DA_QK, (h + 1) * DA_QK)
                src = pl.ds(DA_HEADS * c * tk + h, tk, stride=DA_HEADS)
                kb[rows, hc] = k_ref[0, 0, src, :].astype(BF16)
                vt[hc, rows] = v_ref[0, 0, src, :].T.astype(BF16)
        rel = (lax.broadcasted_iota(jnp.int32, (tk, tq), 1) - lax.broadcasted_iota(jnp.int32, (tk, tq), 0)).astype(F32)
        for h in range(DA_HEADS):
            nrel[h] = -coef[h] * rel

    lane = lax.broadcasted_iota(jnp.int32, (tq, DA_QK), 1)
    zero = jnp.zeros((tq, DA_QK), BF16)
    for h in range(DA_HEADS):
        qh = q_ref[0, :, h * DA_QK:(h + 1) * DA_QK]
        qs[2 * h] = jnp.where(lane < DA_HD, qh, zero)
        qs[2 * h + 1] = jnp.where(lane >= DA_HD, qh, zero)

    mpart[...] = jnp.full(mpart.shape, -jnp.inf, F32)
    groups = tk // SUBLANES

    def shift(kj, h):
        return (-coef[h] * tq) * (qi - kj).astype(F32)

    def scores(kj, masked, heads):
        off = pl.multiple_of(kj * tk, tk)
        if masked:
            krow = lax.broadcasted_iota(jnp.int32, (tk, tq), 0)
            qcol = lax.broadcasted_iota(jnp.int32, (tk, tq), 1)
            visible = (krow // CHUNK) <= (qcol // CHUNK)
        for h in heads:
            kblk = kb[pl.ds(off, tk), h * DA_QK:(h + 1) * DA_QK]
            bias = nrel[h]
            if masked:
                bias = jnp.minimum(bias, -bias)
            for c in range(2):
                i = 2 * h + c
                s = _dot_nt(kblk, qs[i]) + bias
                if masked:
                    s = jnp.where(visible, s, -jnp.inf)
                sc[i, pl.ds(off, tk), :] = s
                blk_max = jnp.max(s.reshape(groups, SUBLANES, tq), axis=0)
                mpart[i] = jnp.maximum(mpart[i], blk_max if masked else blk_max + shift(kj, h))

    def finish_maxima(heads):
        for i in range(2 * heads[0], 2 * heads[-1] + 2):
            mpart[i] = jnp.broadcast_to(jnp.max(mpart[i], axis=0, keepdims=True), (SUBLANES, tq))

    def weights(kj, heads):
        off = pl.multiple_of(kj * tk, tk)
        for h in heads:
            vblk = vt[h * DA_VD:(h + 1) * DA_VD, pl.ds(off, tk)]
            for c in range(2):
                i = 2 * h + c
                s = sc[i, pl.ds(off, tk), :].reshape(groups, SUBLANES, tq)
                p = jnp.exp2(s - (mpart[i] - shift(kj, h))[None])
                lpart[i] += jnp.sum(p, axis=0)
                acc[i] += _dot(vblk, p.reshape(tk, tq).astype(BF16))

    heads = tuple(range(DA_HEADS))
    lpart[...] = jnp.zeros(lpart.shape, F32)
    acc[...] = jnp.zeros(acc.shape, F32)
    _loop_pairs(qi, lambda kj: scores(kj, False, heads))
    scores(qi, True, heads)
    finish_maxima(heads)
    _loop_pairs(qi + 1, lambda kj: weights(kj, heads))

    lam = _lambda(lp_ref, lam_init)
    for h in range(DA_HEADS):
        hc = slice(h * DA_VD, (h + 1) * DA_VD)
        l0 = jnp.sum(lpart[2 * h], axis=0, keepdims=True)
        l1 = jnp.sum(lpart[2 * h + 1], axis=0, keepdims=True)
        o = acc[2 * h] * (1.0 / l0) - acc[2 * h + 1] * (lam / l1)
        o = o * lax.rsqrt(jnp.mean(o * o, axis=0, keepdims=True) + LN_EPS)
        o_ref[0, :, hc] = (o.T * g_ref[:, hc] * (1.0 - lam_init)).astype(BF16)


def _attn_prompt(q, k, v, lam_p, gain, *, layer, lam_init, tq):
    b, t, _ = q.shape
    kv = pl.BlockSpec((1, 1, DA_HEADS * t, DA_QK), lambda i, j: (layer, i, 0, 0))
    tok = pl.BlockSpec((1, tq, DA_WIDTH), lambda i, j: (i, j, 0))
    return pl.pallas_call(
        functools.partial(_attn_prompt_kernel, tq=tq, t=t, lam_init=lam_init),
        grid=(b, t // tq),
        in_specs=[tok, kv, kv, _const_spec(lam_p.shape), _const_spec(gain.shape)],
        out_specs=tok,
        out_shape=jax.ShapeDtypeStruct((b, t, DA_WIDTH), BF16),
        scratch_shapes=[
            pltpu.VMEM((t, DA_WIDTH), BF16),
            pltpu.VMEM((DA_WIDTH, t), BF16),
            pltpu.VMEM((DA_HEADS, tq, tq), F32),
            pltpu.VMEM((2 * DA_HEADS, tq, DA_QK), BF16),
            pltpu.VMEM((2 * DA_HEADS, t, tq), F32),
            pltpu.VMEM((2 * DA_HEADS, DA_VD, tq), F32),
            pltpu.VMEM((2 * DA_HEADS, SUBLANES, tq), F32),
            pltpu.VMEM((2 * DA_HEADS, SUBLANES, tq), F32),
        ],
        compiler_params=_params("parallel", "arbitrary"),
        name="diff_attn_prompt",
    )(q, k, v, lam_p, gain)


def _attn_sample_kernel(q_ref, k_ref, v_ref, ck_ref, cv_ref, lp_ref, g_ref, o_ref, *, past, ts, lam_init):
    lam = _lambda(lp_ref, lam_init)

    def bias_mask(n_keys, key_base):
        qpos = past + lax.broadcasted_iota(jnp.int32, (ts, n_keys), 0)
        kpos = key_base + lax.broadcasted_iota(jnp.int32, (ts, n_keys), 1)
        return jnp.abs(qpos - kpos).astype(F32), (kpos // CHUNK) <= (qpos // CHUNK)

    dist_c, vis_c = bias_mask(past, 0)
    dist_n, vis_n = bias_mask(ts, past)
    lane = lax.broadcasted_iota(jnp.int32, (ts, DA_QK), 1)
    zero = jnp.zeros((ts, DA_QK), BF16)
    for h in range(DA_HEADS):
        hc = slice(h * DA_QK, (h + 1) * DA_QK)
        q = q_ref[0, :, hc]
        qc = (jnp.where(lane < DA_HD, q, zero), jnp.where(lane >= DA_HD, q, zero))
        coef = -LOG2E * ALIBI_SLOPES[h]
        old = pl.ds(h, past, stride=DA_HEADS)
        new = pl.ds(h, ts, stride=DA_HEADS)
        ck = ck_ref[0, 0, old, :].astype(BF16)
        cv = cv_ref[0, 0, old, :].astype(BF16)
        kn = k_ref[0, 0, new, :].astype(BF16)
        vn = v_ref[0, 0, new, :].astype(BF16)
        outs = []
        for c in range(2):
            s_c = jnp.where(vis_c, _dot_nt(qc[c], ck) + coef * dist_c, -jnp.inf)
            s_n = jnp.where(vis_n, _dot_nt(qc[c], kn) + coef * dist_n, -jnp.inf)
            m = jnp.maximum(jnp.max(s_c, axis=1, keepdims=True), jnp.max(s_n, axis=1, keepdims=True))
            p_c = jnp.exp2(s_c - m)
            p_n = jnp.exp2(s_n - m)
            l = jnp.sum(p_c, axis=1, keepdims=True) + jnp.sum(p_n, axis=1, keepdims=True)
            outs.append((_dot(p_c.astype(BF16), cv) + _dot(p_n.astype(BF16), vn)) / l)
        o = outs[0] - lam * outs[1]
        o = o * lax.rsqrt(jnp.mean(o * o, axis=1, keepdims=True) + LN_EPS)
        o_ref[0, :, hc] = (o * g_ref[:, hc] * (1.0 - lam_init)).astype(BF16)


def _attn_sample(q, k, v, cache_k, cache_v, lam_p, gain, *, layer, lam_init):
    b, ts, _ = q.shape
    past = cache_k.shape[2] // DA_HEADS
    tok = pl.BlockSpec((1, ts, DA_WIDTH), lambda i: (i, 0, 0))
    new = pl.BlockSpec((1, 1, DA_HEADS * ts, DA_QK), lambda i: (layer, i, 0, 0))
    old = pl.BlockSpec((1, 1, DA_HEADS * past, DA_QK), lambda i: (layer, i, 0, 0))
    return pl.pallas_call(
        functools.partial(_attn_sample_kernel, past=past, ts=ts, lam_init=lam_init),
        grid=(b,),
        in_specs=[tok, new, new, old, old, _const_spec(lam_p.shape), _const_spec(gain.shape)],
        out_specs=tok,
        out_shape=jax.ShapeDtypeStruct((b, ts, DA_WIDTH), BF16),
        compiler_params=_params("parallel"),
        name="diff_attn_sample",
    )(q, k, v, cache_k, cache_v, lam_p, gain)


def _split3(x):
    hi = x.astype(BF16)
    r = x - hi.astype(F32)
    mid = r.astype(BF16)
    lo = (r - mid.astype(F32)).astype(BF16)
    return hi, mid, lo


def _sublane_groups(x):
    return x.reshape(x.shape[0] // SUBLANES, SUBLANES, x.shape[1])


def _all_sublanes(x8, op):
    for shift in (4, 2, 1):
        x8 = op(x8, pltpu.roll(x8, shift, 0))
    return x8


def _mlstm_kernel(mqk_ref, mv_ref, mo_ref, gc_ref, gr_ref, cw_ref, cb_ref, bc_ref, br_ref, nw_ref,
                  c0_ref, n0_ref, m0_ref, cv0_ref,
                  o_ref, cf_ref, nf_ref, mf_ref, cvf_ref,
                  ubuf, qk_s, c_s, n_s, m_s, *, chunk, tb):
    t = pl.program_id(1)
    n_t = pl.num_programs(1)
    width = 2 * M_WIDTH

    @pl.when(t == 0)
    def _():
        ubuf[0:SUBLANES, :] = cv0_ref[0]
        c_s[...] = c0_ref[0]
        n_s[...] = n0_ref[0]
        m_s[...] = m0_ref[0]

    ubuf[SUBLANES:SUBLANES + tb, :] = mqk_ref[0]
    k_scale = M_HD ** -0.5
    for j in range(width // LANES):
        cols = slice(j * LANES, (j + 1) * LANES)
        y = cb_ref[:, cols]
        for tap in range(CONV_W):
            start = SUBLANES - (CONV_W - 1) + tap
            y = y + ubuf[start:start + tb, cols] * cw_ref[tap:tap + 1, cols]
        y = jax.nn.silu(y)
        if j >= M_WIDTH // LANES:
            y = y * k_scale
        qk_s[:, cols] = y.astype(BF16)
    ubuf[0:SUBLANES, :] = ubuf[tb:tb + SUBLANES, :]

    row_i = lax.broadcasted_iota(jnp.int32, (chunk, chunk), 0)
    col_i = lax.broadcasted_iota(jnp.int32, (chunk, chunk), 1)
    tril = (col_i <= row_i).astype(BF16)
    triu = (row_i <= col_i).astype(BF16)
    seen = row_i <= col_i
    last = chunk - 1

    for ci in range(tb // chunk):
        rows = slice(ci * chunk, (ci + 1) * chunk)
        gcb = gc_ref[0, rows, :] + bc_ref[...]
        grb = gr_ref[0, ci] + br_ref[...]
        b_cols = sum(_dot(tril, part) for part in _split3(jax.nn.log_sigmoid(gcb)))
        b_rows = sum(_dot(part, triu) for part in _split3(jax.nn.log_sigmoid(grb)))
        g_cols = gcb - pltpu.roll(b_cols, LANES - M_HEADS, 1)
        n_bf = n_s[...].astype(BF16)

        for h in range(M_HEADS):
            hc = slice(h * M_HD, (h + 1) * M_HD)
            q = qk_s[rows, hc]
            k = qk_s[rows, M_WIDTH + h * M_HD:M_WIDTH + (h + 1) * M_HD]
            v = mv_ref[0, rows, hc]
            b_row = b_rows[M_HEADS + h:M_HEADS + h + 1, :]
            m_prev = m_s[h:h + 1, 0:1]
            c_prev = c_s[h]
            n_prev = n_s[h:h + 1, :]

            g_b = jnp.broadcast_to(g_cols[:, h:h + 1], (chunk, LANES))
            g_m = _sublane_groups(jnp.where(seen, g_b[:, :chunk], -jnp.inf))
            g_max = _all_sublanes(jnp.max(g_m, axis=0), jnp.maximum)
            p_t = _sublane_groups(_dot_nt(k, q)) * jnp.exp(g_m - g_max[None])
            r = _all_sublanes(jnp.sum(p_t, axis=0), jnp.add)
            u_t = _dot_tn(v, p_t.reshape(chunk, chunk).astype(BF16))
            g_last = g_max[0:1, last:]
            ws_b = jnp.exp(g_b - g_last)
            dc = _dot_tn((v.astype(F32) * ws_b).astype(BF16), k)
            dn = jnp.sum(ws_b * k.astype(F32), axis=0, keepdims=True)

            x = g_max - m_prev
            e1 = jnp.exp(jnp.minimum(x, 0.0))
            e2 = jnp.exp(jnp.minimum(-x, 0.0))
            m_t = b_row + jnp.maximum(m_prev, g_max)
            qc_t = _dot_nt(c_prev.astype(BF16), q)
            qn = _dot_nt(n_bf, q)[h:h + 1, :]
            den = e1 * r + e2 * qn
            inv = 1.0 / jnp.maximum(jnp.abs(den), jnp.exp(-m_t))
            h_t = (e1[None] * _sublane_groups(u_t) + e2[None] * _sublane_groups(qc_t)) * inv[None]
            f, wc = e1[0:1, last:], e2[0:1, last:]
            c_s[h] = wc * c_prev + f * dc
            n_s[h:h + 1, :] = wc * n_prev + f * dn
            m_s[h:h + 1, :] = jnp.broadcast_to(b_row[:, last:] + jnp.maximum(m_prev, g_last), (1, LANES))

            hcen = h_t - (_all_sublanes(jnp.sum(h_t, axis=0), jnp.add) * (1.0 / M_HD))[None]
            var = _all_sublanes(jnp.sum(hcen * hcen, axis=0), jnp.add) * (1.0 / M_HD)
            hn = (hcen * lax.rsqrt(var + LN_EPS)[None]).reshape(M_HD, chunk).T
            o_ref[0, rows, hc] = (hn * nw_ref[:, hc] * jax.nn.sigmoid(mo_ref[0, rows, hc])).astype(BF16)

    @pl.when(t == n_t - 1)
    def _():
        cf_ref[0] = c_s[...]
        nf_ref[0] = n_s[...]
        mf_ref[0] = m_s[...]
        cvf_ref[0] = ubuf[0:SUBLANES, :]


def _mlstm(mqk, mv, mo, gc, gr, conv_w, conv_b, b_if, norm_w, c0, n0, m0, cv0, *, chunk, tb):
    b, t, width = mqk.shape
    tok = lambda n: pl.BlockSpec((1, tb, n), lambda i, j: (i, j, 0))
    per_b = lambda shape: pl.BlockSpec((1,) + shape, lambda i, j: (i,) + (0,) * len(shape))
    bc = jnp.zeros((1, LANES), F32).at[0, :N_GATES].set(b_if)
    br = b_if.reshape(N_GATES, 1)
    out_shape = (
        jax.ShapeDtypeStruct((b, t, M_WIDTH), BF16),
        jax.ShapeDtypeStruct((b, M_HEADS, M_HD, M_HD), F32),
        jax.ShapeDtypeStruct((b, SUBLANES, M_HD), F32),
        jax.ShapeDtypeStruct((b, SUBLANES, LANES), F32),
        jax.ShapeDtypeStruct((b, SUBLANES, width), F32),
    )
    return pl.pallas_call(
        functools.partial(_mlstm_kernel, chunk=chunk, tb=tb),
        grid=(b, t // tb),
        in_specs=[
            tok(width), tok(M_WIDTH), tok(M_WIDTH), tok(LANES),
            pl.BlockSpec((1, tb // chunk, N_GATES, chunk), lambda i, j: (i, j, 0, 0)),
            _const_spec(conv_w.shape), _const_spec((1, width)), _const_spec(bc.shape), _const_spec(br.shape),
            _const_spec((1, M_WIDTH)),
            per_b((M_HEADS, M_HD, M_HD)), per_b((SUBLANES, M_HD)), per_b((SUBLANES, LANES)), per_b((SUBLANES, width)),
        ],
        out_specs=(tok(M_WIDTH), per_b((M_HEADS, M_HD, M_HD)), per_b((SUBLANES, M_HD)),
                   per_b((SUBLANES, LANES)), per_b((SUBLANES, width))),
        out_shape=out_shape,
        scratch_shapes=[
            pltpu.VMEM((SUBLANES + tb, width), F32),
            pltpu.VMEM((tb, width), BF16),
            pltpu.VMEM((M_HEADS, M_HD, M_HD), F32),
            pltpu.VMEM((SUBLANES, M_HD), F32),
            pltpu.VMEM((SUBLANES, LANES), F32),
        ],
        compiler_params=_params("parallel", "arbitrary"),
        name="mlstm",
    )(mqk, mv, mo, gc, gr, conv_w, conv_b.reshape(1, width), bc, br, norm_w.reshape(1, M_WIDTH), c0, n0, m0, cv0)


def _mix_kernel(x_ref, an_ref, mn_ref, sc_ref, sh_ref, g1_ref, wa_ref, wb_ref, wg_ref, bg_ref, wo_ref,
                lg_ref, lb_ref, o_ref, *, alpha):
    d = x_ref.shape[-1]
    for r in _row_parts(x_ref.shape[1]):
        x = x_ref[0, r, :]
        h = (_layer_norm(x) * (1.0 + _mod_rows(sc_ref, r)) + _mod_rows(sh_ref, r)).astype(BF16)
        y_a = _dot(an_ref[0, r, :], wa_ref[...])
        y_b = _dot(mn_ref[0, r, :], wb_ref[...])
        g_a = jax.nn.sigmoid(_dot(h, wg_ref[:, :d]) + bg_ref[:, :d])
        g_b = jax.nn.sigmoid(_dot(h, wg_ref[:, d:]) + bg_ref[:, d:])
        mix = _dot((g_a * y_a + g_b * y_b).astype(BF16), wo_ref[...])
        o_ref[0, r, :] = _layer_norm(alpha * x + (1.0 + _mod_rows(g1_ref, r)) * mix) * lg_ref[...] + lb_ref[...]


def _mix(x, an, mn, sc, sh, g1, w_a, w_b, w_g, b_g, w_o, ln_g, ln_b, *, tm, alpha):
    b, t, d = x.shape
    tok = lambda n: pl.BlockSpec((1, tm, n), lambda i, j: (i, j, 0))
    row = _mod_spec(sc, tm)
    return pl.pallas_call(
        functools.partial(_mix_kernel, alpha=alpha),
        grid=(b, t // tm),
        in_specs=[tok(d), tok(DA_WIDTH), tok(M_WIDTH), row, row, row,
                  _const_spec(w_a.shape), _const_spec(w_b.shape), _const_spec(w_g.shape), _const_spec((1, 2 * d)),
                  _const_spec(w_o.shape), _const_spec((1, d)), _const_spec((1, d))],
        out_specs=tok(d),
        out_shape=jax.ShapeDtypeStruct((b, t, d), F32),
        compiler_params=_params("parallel", "parallel"),
        name="mix_out",
    )(x, an, mn, sc, sh, g1, w_a, w_b, w_g, b_g.reshape(1, 2 * d), w_o, ln_g.reshape(1, d), ln_b.reshape(1, d))


def _ffn_kernel(x_ref, sc_ref, sh_ref, g2_ref, wgu_ref, wd_ref, lg_ref, lb_ref, o_ref, *, alpha, d_ff, fc):
    rows = x_ref.shape[1]
    n_parts = max(rows // FFN_PART_ROWS, 1)
    for p in range(n_parts):
        r = slice(p * rows // n_parts, (p + 1) * rows // n_parts)
        h = (_layer_norm(x_ref[0, r, :]) * (1.0 + _mod_rows(sc_ref, r)) + _mod_rows(sh_ref, r)).astype(BF16)
        for j in range(d_ff // fc):
            gt = _dot(h, wgu_ref[:, j * fc:(j + 1) * fc])
            up = _dot(h, wgu_ref[:, d_ff + j * fc:d_ff + (j + 1) * fc])
            part = _dot((jax.nn.silu(gt) * up).astype(BF16), wd_ref[j * fc:(j + 1) * fc, :])
            if j == 0:
                o_ref[0, r, :] = part
            else:
                o_ref[0, r, :] += part
        y = alpha * x_ref[0, r, :] + (1.0 + _mod_rows(g2_ref, r)) * o_ref[0, r, :]
        o_ref[0, r, :] = _layer_norm(y) * lg_ref[...] + lb_ref[...]


def _ffn(x, sc, sh, g2, w_gu, w_down, ln_g, ln_b, *, tm, alpha):
    b, t, d = x.shape
    d_ff = w_down.shape[0]
    fc = 256
    tok = pl.BlockSpec((1, tm, d), lambda i, j: (i, j, 0))
    row = _mod_spec(sc, tm)
    return pl.pallas_call(
        functools.partial(_ffn_kernel, alpha=alpha, d_ff=d_ff, fc=fc),
        grid=(b, t // tm),
        in_specs=[tok, row, row, row, _const_spec(w_gu.shape), _const_spec(w_down.shape),
                  _const_spec((1, d)), _const_spec((1, d))],
        out_specs=tok,
        out_shape=jax.ShapeDtypeStruct((b, t, d), F32),
        compiler_params=_params("parallel", "parallel"),
        name="swiglu",
    )(x, sc, sh, g2, w_gu, w_down, ln_g.reshape(1, d), ln_b.reshape(1, d))


def _tile(t, pref):
    return pref if t % pref == 0 else t


def _layer(x, mod, layer, depth, attend, mstate, weights, kv_prev):
    (w_a, w_if, w_ift, b_if, conv_w, conv_b, lam_p, da_norm_w, m_norm_w, w_br_a, w_br_b, w_gate, b_gate, w_o,
     ln1_g, ln1_b, w_gu, w_down, ln2_g, ln2_b) = weights
    b, t, d = x.shape
    alpha = (2 * depth) ** 0.25
    lam_init = 0.8 - 0.6 * math.exp(-0.3 * layer)
    mods = [m.reshape(b, 1, d) for m in jnp.split(mod, 6, axis=-1)]
    chunk = min(CHUNK, t)
    tb = _tile(t, 512)
    flat = t % LANES != 0
    fb, ft = (1, b * t) if flat else (b, t)
    if flat:
        mods = [jnp.broadcast_to(m, (b, t, d)).reshape(fb, ft, d) for m in mods]
    sh1, sc1, g1, sh2, sc2, g2 = mods
    tm = _tile(ft, 512)
    tokens = lambda a: a.reshape(fb, ft, a.shape[-1])
    sequences = lambda a: a.reshape(b, t, a.shape[-1])

    q, k, v, mqk, mv, mo, gc, gr = _inproj(tokens(x), sc1, sh1, w_a, w_if, w_ift, kv_prev,
                                           layer=layer, depth=depth, tm=tm, chunk=chunk)
    k_seq, v_seq = (a.reshape(depth, b, DA_HEADS * t, DA_QK) for a in (k, v))
    a_n = attend(sequences(q), k_seq, v_seq, lam_p, da_norm_w.reshape(1, DA_WIDTH), layer, lam_init)
    m_n, c_f, n_f, m_f, cv_f = _mlstm(sequences(mqk), sequences(mv), sequences(mo), sequences(gc),
                                      gr.reshape(b, t // chunk, N_GATES, chunk), conv_w, conv_b, b_if, m_norm_w,
                                      *mstate, chunk=chunk, tb=tb)
    y = _mix(tokens(x), tokens(a_n), tokens(m_n), sc1, sh1, g1, w_br_a, w_br_b, w_gate, b_gate, w_o, ln1_g, ln1_b,
             tm=tm, alpha=alpha)
    y = _ffn(y, sc2, sh2, g2, w_gu, w_down, ln2_g, ln2_b, tm=_tile(ft, 2 * FFN_PART_ROWS), alpha=alpha)
    state = (c_f, n_f[:, :M_HEADS], m_f[:, :M_HEADS, 0], cv_f[:, SUBLANES - (CONV_W - 1):])
    return sequences(y), (k, v), state


def _pad_rows(a, rows):
    pad = [(0, 0)] * a.ndim
    pad[1] = (rows - a.shape[1], 0)
    return jnp.pad(a, pad)


def kernel(x_prompt, x_sample, c_prompt, c_sample, cache_attn_k, cache_attn_v, state_mlstm_C, state_mlstm_n,
           state_mlstm_m, state_mlstm_conv, w_ada, b_ada, w_in, b_if, conv_w, conv_b, lam_p, da_norm_w, m_norm_w,
           w_br_a, w_br_b, w_gate, b_gate, w_o, ln1_g, ln1_b, w_gu, w_down, ln2_g, ln2_b):
    depth = w_in.shape[0]
    bp, bs = x_prompt.shape[0], x_sample.shape[0]
    past = cache_attn_k.shape[2]

    mod = _ada(jnp.concatenate([c_prompt, c_sample], axis=0), w_ada.astype(BF16), b_ada)

    gate_lo = 3 * DA_WIDTH + 2 * M_WIDTH + M_WIDTH
    gate_hi = gate_lo + N_GATES

    cache_k = cache_attn_k.reshape(depth, bs, past * DA_HEADS, DA_QK)
    cache_v = cache_attn_v.reshape(depth, bs, past * DA_HEADS, DA_VD)

    xp, xs = x_prompt, x_sample
    kv_p = kv_s = None
    outs_p, outs_s = [], []
    for l in range(depth):
        w_l = w_in[l]
        w_a = jnp.concatenate([w_l[:, :gate_lo], w_l[:, gate_hi:]], axis=1).astype(BF16)
        w_if = jnp.pad(w_l[:, gate_lo:gate_hi], ((0, 0), (0, LANES - N_GATES))).astype(BF16)
        w_ift = w_l[:, gate_lo:gate_hi].T.astype(BF16)
        weights = (w_a, w_if, w_ift, b_if[l], conv_w[l], conv_b[l], lam_p[l], da_norm_w[l], m_norm_w[l],
                   w_br_a[l].astype(BF16), w_br_b[l].astype(BF16), w_gate[l].astype(BF16), b_gate[l],
                   w_o[l].astype(BF16), ln1_g[l], ln1_b[l], w_gu[l].astype(BF16), w_down[l].astype(BF16),
                   ln2_g[l], ln2_b[l])

        zero_state = (jnp.zeros((bp, M_HEADS, M_HD, M_HD), F32), jnp.zeros((bp, SUBLANES, M_HD), F32),
                      jnp.zeros((bp, SUBLANES, LANES), F32), jnp.zeros((bp, SUBLANES, 2 * M_WIDTH), F32))
        attend_p = lambda q, k, v, lp, g, layer, lam_init: _attn_prompt(
            q, k, v, lp, g, layer=layer, lam_init=lam_init, tq=_tile(q.shape[1], 256))
        xp, kv_p, st_p = _layer(xp, mod[l, :bp], l, depth, attend_p, zero_state, weights, kv_p)
        outs_p.append(st_p)

        attend_s = lambda q, k, v, lp, g, layer, lam_init: _attn_sample(
            q, k, v, cache_k, cache_v, lp, g, layer=layer, lam_init=lam_init)
        head_pad = ((0, 0), (0, SUBLANES - M_HEADS), (0, 0))
        state_s = (state_mlstm_C[l],
                   jnp.pad(state_mlstm_n[l], head_pad),
                   jnp.pad(jnp.broadcast_to(state_mlstm_m[l][:, :, None], (bs, M_HEADS, LANES)), head_pad),
                   _pad_rows(state_mlstm_conv[l], SUBLANES))
        xs, kv_s, st_s = _layer(xs, mod[l, bp:], l, depth, attend_s, state_s, weights, kv_s)
        outs_s.append(st_s)

    stack = lambda outs, i: jnp.stack([o[i] for o in outs])
    kp, vp = (a.reshape(depth, bp, x_prompt.shape[1], DA_HEADS, DA_QK) for a in kv_p)
    ks, vs = (a.reshape(depth, bs, x_sample.shape[1], DA_HEADS, DA_QK) for a in kv_s)
    cp, np_, mp, cvp = (stack(outs_p, i) for i in range(4))
    cs, ns, ms, cvs = (stack(outs_s, i) for i in range(4))
    return (xp, xs, kp, vp, ks, vs, cp, np_, mp, cvp, cs, ns, ms, cvs)
```

```python
import functools
import math

import jax
import jax.numpy as jnp
from jax import lax
from jax.experimental import pallas as pl
from jax.experimental.pallas import tpu as pltpu

F32 = jnp.float32
BF16 = jnp.bfloat16

D_MODEL = 1024
CHUNK = 64
DA_HEADS = 4
DA_HD = 64
DA_QK = 2 * DA_HD
DA_VD = 2 * DA_HD
DA_WIDTH = DA_HEADS * DA_VD
M_HEADS = 4
M_HD = 128
M_WIDTH = M_HEADS * M_HD
CONV_W = 4
N_GATES = 2 * M_HEADS
LN_EPS = 1e-5
LOG2E = math.log2(math.e)
Q_SCALE = DA_HD ** -0.5 * LOG2E
ALIBI_SLOPES = tuple(2.0 ** (-8.0 * (i + 1) / DA_HEADS) for i in range(DA_HEADS))

SUBLANES = 8
LANES = 128
VMEM_LIMIT_BYTES = 56 * 1024 * 1024

_NT = (((1,), (1,)), ((), ()))
_TN = (((0,), (0,)), ((), ()))


def _dot(a, b):
    return jnp.dot(a, b, preferred_element_type=F32)


def _dot_nt(a, b):
    return lax.dot_general(a, b, _NT, preferred_element_type=F32)


def _dot_tn(a, b):
    return lax.dot_general(a, b, _TN, preferred_element_type=F32)


def _layer_norm(x):
    mu = jnp.mean(x, axis=-1, keepdims=True)
    xc = x - mu
    var = jnp.mean(xc * xc, axis=-1, keepdims=True)
    return xc * lax.rsqrt(var + LN_EPS)


ROW_PARTS = 2
FFN_PART_ROWS = 512


def _row_parts(rows):
    if rows % (ROW_PARTS * 128) != 0:
        return (slice(0, rows),)
    part = rows // ROW_PARTS
    return tuple(slice(i * part, (i + 1) * part) for i in range(ROW_PARTS))


def _mod_rows(ref, r):
    return ref[0] if ref.shape[1] == 1 else ref[0, r, :]


def _mod_spec(mod, tm):
    rows = mod.shape[1]
    if rows == 1:
        return pl.BlockSpec((1, 1, mod.shape[2]), lambda i, j: (i, 0, 0))
    return pl.BlockSpec((1, tm, mod.shape[2]), lambda i, j: (i, j, 0))


def _params(*sem):
    return pltpu.CompilerParams(dimension_semantics=sem, vmem_limit_bytes=VMEM_LIMIT_BYTES)


def _const_spec(shape):
    nd = len(shape)
    return pl.BlockSpec(shape, lambda *_: (0,) * nd, pipeline_mode=pl.Buffered(1))


def _ada_kernel(c_ref, w_ref, b_ref, o_ref):
    s = jax.nn.silu(c_ref[...]).astype(BF16)
    o_ref[0] = _dot(s, w_ref[0]) + b_ref[0]


def _ada(c_all, w_ada, b_ada):
    depth, d, n = w_ada.shape
    r = c_all.shape[0]
    tn = 2048
    return pl.pallas_call(
        _ada_kernel,
        grid=(depth, n // tn),
        in_specs=[
            pl.BlockSpec((r, d), lambda l, j: (0, 0)),
            pl.BlockSpec((1, d, tn), lambda l, j: (l, 0, j)),
            pl.BlockSpec((1, 1, tn), lambda l, j: (l, 0, j)),
        ],
        out_specs=pl.BlockSpec((1, r, tn), lambda l, j: (l, 0, j)),
        out_shape=jax.ShapeDtypeStruct((depth, r, n), F32),
        compiler_params=_params("parallel", "parallel"),
        name="ada_mod",
    )(c_all, w_ada, b_ada.reshape(depth, 1, n))


def _inproj_kernel(x_ref, sc_ref, sh_ref, w_ref, wif_ref, wift_ref, *refs, chunk):
    q_ref, k_ref, v_ref, mqk_ref, mv_ref, mo_ref, gc_ref, gr_ref = refs[-8:]
    for r in _row_parts(x_ref.shape[1]):
        n_rows = r.stop - r.start
        h = (_layer_norm(x_ref[0, r, :]) * (1.0 + _mod_rows(sc_ref, r)) + _mod_rows(sh_ref, r)).astype(BF16)

        def proj(lo, hi):
            return _dot(h, w_ref[:, lo:hi])

        q_ref[0, r, :] = (proj(0, 512) * Q_SCALE).astype(BF16)
        for out_ref, lo in ((k_ref, DA_WIDTH), (v_ref, 2 * DA_WIDTH)):
            rows = proj(lo, lo + DA_WIDTH)
            for hd in range(DA_HEADS):
                dst = pl.ds(DA_HEADS * r.start + hd, n_rows, stride=DA_HEADS)
                out_ref[0, 0, dst, :] = rows[:, hd * DA_QK:(hd + 1) * DA_QK]
        mqk_ref[0, r, :] = proj(1536, 2560)
        mv_ref[0, r, :] = proj(2560, 3072).astype(BF16)
        mo_ref[0, r, :] = proj(3072, 3584)
        gc_ref[0, r, :] = _dot(h, wif_ref[...])
        gr = _dot_nt(wift_ref[...], h)
        for j in range(n_rows // chunk):
            gr_ref[0, r.start // chunk + j] = gr[:, j * chunk:(j + 1) * chunk]


def _inproj(x, sc, sh, w_a, w_if, w_ift, kv_prev, *, layer, depth, tm, chunk):
    b, t, d = x.shape
    n_chunks = tm // chunk
    tok = lambda n: pl.BlockSpec((1, tm, n), lambda i, j: (i, j, 0))
    row = _mod_spec(sc, tm)
    kv_shape = jax.ShapeDtypeStruct((depth, b, DA_HEADS * t, DA_QK), F32)
    kv_spec = pl.BlockSpec((1, 1, DA_HEADS * tm, DA_QK), lambda i, j: (layer, i, j, 0))
    n_in = 6
    aliased = () if kv_prev is None else tuple(kv_prev)
    out_shape = (
        jax.ShapeDtypeStruct((b, t, DA_WIDTH), BF16),
        kv_shape,
        kv_shape,
        jax.ShapeDtypeStruct((b, t, 2 * M_WIDTH), F32),
        jax.ShapeDtypeStruct((b, t, M_WIDTH), BF16),
        jax.ShapeDtypeStruct((b, t, M_WIDTH), F32),
        jax.ShapeDtypeStruct((b, t, LANES), F32),
        jax.ShapeDtypeStruct((b, t // chunk, N_GATES, chunk), F32),
    )
    out_specs = (
        tok(DA_WIDTH), kv_spec, kv_spec, tok(2 * M_WIDTH), tok(M_WIDTH), tok(M_WIDTH), tok(LANES),
        pl.BlockSpec((1, n_chunks, N_GATES, chunk), lambda i, j: (i, j, 0, 0)),
    )
    return pl.pallas_call(
        functools.partial(_inproj_kernel, chunk=chunk),
        grid=(b, t // tm),
        in_specs=[tok(d), row, row, _const_spec(w_a.shape), _const_spec(w_if.shape), _const_spec(w_ift.shape)]
        + [pl.BlockSpec(memory_space=pl.ANY)] * len(aliased),
        out_specs=out_specs,
        out_shape=out_shape,
        input_output_aliases={n_in + a: 1 + a for a in range(len(aliased))},
        compiler_params=_params("parallel", "parallel"),
        name="in_proj",
    )(x, sc, sh, w_a, w_if, w_ift, *aliased)


def _lambda(lp_ref, lam_init):
    lp = lp_ref[...]
    a = jnp.sum(lp[0:1] * lp[1:2], axis=1, keepdims=True)
    b = jnp.sum(lp[2:3] * lp[3:4], axis=1, keepdims=True)
    return jnp.exp(a) - jnp.exp(b) + lam_init


def _loop_pairs(n, step):
    def pair(j, carry):
        step(2 * j)
        step(2 * j + 1)
        return carry

    lax.fori_loop(0, lax.shift_right_logical(n, 1), pair, 0)

    @pl.when(lax.bitwise_and(n, 1) == 1)
    def _():
        step(n - 1)


def _attn_prompt_kernel(q_ref, k_ref, v_ref, lp_ref, g_ref, o_ref,
                        kb, vt, nrel, qs, sc, acc, mpart, lpart, *, tq, t, lam_init):
    qi = pl.program_id(1)
    tk = tq
    coef = [s * LOG2E for s in ALIBI_SLOPES]

    @pl.when(qi == 0)
    def _():
        for c in range(t // tk):
            rows = slice(c * tk, (c + 1) * tk)
            for h in range(DA_HEADS):
                hc = slice(h * DA_QK, (h + 1) * DA_QK)
                src = pl.ds(DA_HEADS * c * tk + h, tk, stride=DA_HEADS)
                kb[rows, hc] = k_ref[0, 0, src, :].astype(BF16)
                vt[hc, rows] = v_ref[0, 0, src, :].T.astype(BF16)
        krow = lax.broadcasted_iota(jnp.int32, (tk, tq), 0)
        qcol = lax.broadcasted_iota(jnp.int32, (tk, tq), 1)
        rel = (qcol - krow).astype(F32)
        visible = (krow // CHUNK) <= (qcol // CHUNK)
        for h in range(DA_HEADS):
            nrel[h] = -coef[h] * rel
            nrel[DA_HEADS + h] = jnp.where(visible, -coef[h] * jnp.abs(rel), -jnp.inf)

    lane = lax.broadcasted_iota(jnp.int32, (tq, DA_QK), 1)
    zero = jnp.zeros((tq, DA_QK), BF16)
    for h in range(DA_HEADS):
        qh = q_ref[0, :, h * DA_QK:(h + 1) * DA_QK]
        qs[2 * h] = jnp.where(lane < DA_HD, qh, zero)
        qs[2 * h + 1] = jnp.where(lane >= DA_HD, qh, zero)

    mpart[...] = jnp.full(mpart.shape, -jnp.inf, F32)
    groups = tk // SUBLANES

    def shift(kj, h):
        return (-coef[h] * tq) * (qi - kj).astype(F32)

    def scores(kj, heads):
        off = pl.multiple_of(kj * tk, tk)
        diagonal = kj == qi
        for h in heads:
            kblk = kb[pl.ds(off, tk), h * DA_QK:(h + 1) * DA_QK]
            bias = nrel[jnp.where(diagonal, DA_HEADS + h, h)]
            for c in range(2):
                i = 2 * h + c
                s = _dot_nt(kblk, qs[i]) + bias
                sc[i, pl.ds(off, tk), :] = s
                blk_max = jnp.max(s.reshape(groups, SUBLANES, tq), axis=0)
                mpart[i] = jnp.maximum(mpart[i], blk_max + shift(kj, h))

    def finish_maxima(heads):
        for i in range(2 * heads[0], 2 * heads[-1] + 2):
            mpart[i] = jnp.broadcast_to(jnp.max(mpart[i], axis=0, keepdims=True), (SUBLANES, tq))

    def weights(kj, heads):
        off = pl.multiple_of(kj * tk, tk)
        for h in heads:
            vblk = vt[h * DA_VD:(h + 1) * DA_VD, pl.ds(off, tk)]
            for c in range(2):
                i = 2 * h + c
                s = sc[i, pl.ds(off, tk), :].reshape(groups, SUBLANES, tq)
                p = jnp.exp2(s - (mpart[i] - shift(kj, h))[None])
                lpart[i] += jnp.sum(p, axis=0)
                acc[i] += _dot(vblk, p.reshape(tk, tq).astype(BF16))

    heads = tuple(range(DA_HEADS))
    lpart[...] = jnp.zeros(lpart.shape, F32)
    acc[...] = jnp.zeros(acc.shape, F32)
    _loop_pairs(qi + 1, lambda kj: scores(kj, heads))
    finish_maxima(heads)
    _loop_pairs(qi + 1, lambda kj: weights(kj, heads))

    lam = _lambda(lp_ref, lam_init)
    for h in range(DA_HEADS):
        hc = slice(h * DA_VD, (h + 1) * DA_VD)
        l0 = jnp.sum(lpart[2 * h], axis=0, keepdims=True)
        l1 = jnp.sum(lpart[2 * h + 1], axis=0, keepdims=True)
        o = acc[2 * h] * (1.0 / l0) - acc[2 * h + 1] * (lam / l1)
        o = o * lax.rsqrt(jnp.mean(o * o, axis=0, keepdims=True) + LN_EPS)
        o_ref[0, :, hc] = (o.T * g_ref[:, hc] * (1.0 - lam_init)).astype(BF16)


def _attn_prompt(q, k, v, lam_p, gain, *, layer, lam_init, tq):
    b, t, _ = q.shape
    kv = pl.BlockSpec((1, 1, DA_HEADS * t, DA_QK), lambda i, j: (layer, i, 0, 0))
    tok = pl.BlockSpec((1, tq, DA_WIDTH), lambda i, j: (i, j, 0))
    return pl.pallas_call(
        functools.partial(_attn_prompt_kernel, tq=tq, t=t, lam_init=lam_init),
        grid=(b, t // tq),
        in_specs=[tok, kv, kv, _const_spec(lam_p.shape), _const_spec(gain.shape)],
        out_specs=tok,
        out_shape=jax.ShapeDtypeStruct((b, t, DA_WIDTH), BF16),
        scratch_shapes=[
            pltpu.VMEM((t, DA_WIDTH), BF16),
            pltpu.VMEM((DA_WIDTH, t), BF16),
            pltpu.VMEM((2 * DA_HEADS, tq, tq), F32),
            pltpu.VMEM((2 * DA_HEADS, tq, DA_QK), BF16),
            pltpu.VMEM((2 * DA_HEADS, t, tq), F32),
            pltpu.VMEM((2 * DA_HEADS, DA_VD, tq), F32),
            pltpu.VMEM((2 * DA_HEADS, SUBLANES, tq), F32),
            pltpu.VMEM((2 * DA_HEADS, SUBLANES, tq), F32),
        ],
        compiler_params=_params("parallel", "arbitrary"),
        name="diff_attn_prompt",
    )(q, k, v, lam_p, gain)


def _attn_sample_kernel(q_ref, k_ref, v_ref, ck_ref, cv_ref, lp_ref, g_ref, o_ref, *, past, ts, lam_init):
    lam = _lambda(lp_ref, lam_init)

    def bias_mask(n_keys, key_base):
        qpos = past + lax.broadcasted_iota(jnp.int32, (ts, n_keys), 0)
        kpos = key_base + lax.broadcasted_iota(jnp.int32, (ts, n_keys), 1)
        return jnp.abs(qpos - kpos).astype(F32), (kpos // CHUNK) <= (qpos // CHUNK)

    dist_c, vis_c = bias_mask(past, 0)
    dist_n, vis_n = bias_mask(ts, past)
    lane = lax.broadcasted_iota(jnp.int32, (ts, DA_QK), 1)
    zero = jnp.zeros((ts, DA_QK), BF16)
    for h in range(DA_HEADS):
        hc = slice(h * DA_QK, (h + 1) * DA_QK)
        q = q_ref[0, :, hc]
        qc = (jnp.where(lane < DA_HD, q, zero), jnp.where(lane >= DA_HD, q, zero))
        coef = -LOG2E * ALIBI_SLOPES[h]
        old = pl.ds(h, past, stride=DA_HEADS)
        new = pl.ds(h, ts, stride=DA_HEADS)
        ck = ck_ref[0, 0, old, :].astype(BF16)
        cv = cv_ref[0, 0, old, :].astype(BF16)
        kn = k_ref[0, 0, new, :].astype(BF16)
        vn = v_ref[0, 0, new, :].astype(BF16)
        outs = []
        for c in range(2):
            s_c = jnp.where(vis_c, _dot_nt(qc[c], ck) + coef * dist_c, -jnp.inf)
            s_n = jnp.where(vis_n, _dot_nt(qc[c], kn) + coef * dist_n, -jnp.inf)
            m = jnp.maximum(jnp.max(s_c, axis=1, keepdims=True), jnp.max(s_n, axis=1, keepdims=True))
            p_c = jnp.exp2(s_c - m)
            p_n = jnp.exp2(s_n - m)
            l = jnp.sum(p_c, axis=1, keepdims=True) + jnp.sum(p_n, axis=1, keepdims=True)
            outs.append((_dot(p_c.astype(BF16), cv) + _dot(p_n.astype(BF16), vn)) / l)
        o = outs[0] - lam * outs[1]
        o = o * lax.rsqrt(jnp.mean(o * o, axis=1, keepdims=True) + LN_EPS)
        o_ref[0, :, hc] = (o * g_ref[:, hc] * (1.0 - lam_init)).astype(BF16)


def _attn_sample(q, k, v, cache_k, cache_v, lam_p, gain, *, layer, lam_init):
    b, ts, _ = q.shape
    past = cache_k.shape[2] // DA_HEADS
    tok = pl.BlockSpec((1, ts, DA_WIDTH), lambda i: (i, 0, 0))
    new = pl.BlockSpec((1, 1, DA_HEADS * ts, DA_QK), lambda i: (layer, i, 0, 0))
    old = pl.BlockSpec((1, 1, DA_HEADS * past, DA_QK), lambda i: (layer, i, 0, 0))
    return pl.pallas_call(
        functools.partial(_attn_sample_kernel, past=past, ts=ts, lam_init=lam_init),
        grid=(b,),
        in_specs=[tok, new, new, old, old, _const_spec(lam_p.shape), _const_spec(gain.shape)],
        out_specs=tok,
        out_shape=jax.ShapeDtypeStruct((b, ts, DA_WIDTH), BF16),
        compiler_params=_params("parallel"),
        name="diff_attn_sample",
    )(q, k, v, cache_k, cache_v, lam_p, gain)


def _split3(x):
    hi = x.astype(BF16)
    r = x - hi.astype(F32)
    mid = r.astype(BF16)
    lo = (r - mid.astype(F32)).astype(BF16)
    return hi, mid, lo


def _sublane_groups(x):
    return x.reshape(x.shape[0] // SUBLANES, SUBLANES, x.shape[1])


def _all_sublanes(x8, op):
    for shift in (4, 2, 1):
        x8 = op(x8, pltpu.roll(x8, shift, 0))
    return x8


def _mlstm_kernel(mqk_ref, mv_ref, mo_ref, gc_ref, gr_ref, cw_ref, cb_ref, bc_ref, br_ref, nw_ref,
                  c0_ref, n0_ref, m0_ref, cv0_ref,
                  o_ref, cf_ref, nf_ref, mf_ref, cvf_ref,
                  ubuf, qk_s, c_s, n_s, m_s, *, chunk, tb):
    t = pl.program_id(1)
    n_t = pl.num_programs(1)
    width = 2 * M_WIDTH

    @pl.when(t == 0)
    def _():
        ubuf[0:SUBLANES, :] = cv0_ref[0]
        c_s[...] = c0_ref[0]
        n_s[...] = n0_ref[0]
        m_s[...] = m0_ref[0]

    ubuf[SUBLANES:SUBLANES + tb, :] = mqk_ref[0]
    k_scale = M_HD ** -0.5
    for j in range(width // LANES):
        cols = slice(j * LANES, (j + 1) * LANES)
        y = cb_ref[:, cols]
        for tap in range(CONV_W):
            start = SUBLANES - (CONV_W - 1) + tap
            y = y + ubuf[start:start + tb, cols] * cw_ref[tap:tap + 1, cols]
        y = jax.nn.silu(y)
        if j >= M_WIDTH // LANES:
            y = y * k_scale
        qk_s[:, cols] = y.astype(BF16)
    ubuf[0:SUBLANES, :] = ubuf[tb:tb + SUBLANES, :]

    row_i = lax.broadcasted_iota(jnp.int32, (chunk, chunk), 0)
    col_i = lax.broadcasted_iota(jnp.int32, (chunk, chunk), 1)
    tril = (col_i <= row_i).astype(BF16)
    triu = (row_i <= col_i).astype(BF16)
    seen = row_i <= col_i
    last = chunk - 1

    for ci in range(tb // chunk):
        rows = slice(ci * chunk, (ci + 1) * chunk)
        gcb = gc_ref[0, rows, :] + bc_ref[...]
        grb = gr_ref[0, ci] + br_ref[...]
        b_cols = sum(_dot(tril, part) for part in _split3(jax.nn.log_sigmoid(gcb)))
        b_rows = sum(_dot(part, triu) for part in _split3(jax.nn.log_sigmoid(grb)))
        g_cols = gcb - pltpu.roll(b_cols, LANES - M_HEADS, 1)
        n_bf = n_s[...].astype(BF16)

        for h in range(M_HEADS):
            hc = slice(h * M_HD, (h + 1) * M_HD)
            q = qk_s[rows, hc]
            k = qk_s[rows, M_WIDTH + h * M_HD:M_WIDTH + (h + 1) * M_HD]
            v = mv_ref[0, rows, hc]
            b_row = b_rows[M_HEADS + h:M_HEADS + h + 1, :]
            m_prev = m_s[h:h + 1, 0:1]
            c_prev = c_s[h]
            n_prev = n_s[h:h + 1, :]

            g_b = jnp.broadcast_to(g_cols[:, h:h + 1], (chunk, LANES))
            g_m = _sublane_groups(jnp.where(seen, g_b[:, :chunk], -jnp.inf))
            g_max = _all_sublanes(jnp.max(g_m, axis=0), jnp.maximum)
            p_t = _sublane_groups(_dot_nt(k, q)) * jnp.exp(g_m - g_max[None])
            r = _all_sublanes(jnp.sum(p_t, axis=0), jnp.add)
            u_t = _dot_tn(v, p_t.reshape(chunk, chunk).astype(BF16))
            g_last = g_max[0:1, last:]
            ws_b = jnp.exp(g_b - g_last)
            dc = _dot_tn((v.astype(F32) * ws_b).astype(BF16), k)
            dn = jnp.sum(ws_b * k.astype(F32), axis=0, keepdims=True)

            x = g_max - m_prev
            e1 = jnp.exp(jnp.minimum(x, 0.0))
            e2 = jnp.exp(jnp.minimum(-x, 0.0))
            m_t = b_row + jnp.maximum(m_prev, g_max)
            qc_t = _dot_nt(c_prev.astype(BF16), q)
            qn = _dot_nt(n_bf, q)[h:h + 1, :]
            den = e1 * r + e2 * qn
            inv = 1.0 / jnp.maximum(jnp.abs(den), jnp.exp(-m_t))
            h_t = (e1[None] * _sublane_groups(u_t) + e2[None] * _sublane_groups(qc_t)) * inv[None]
            f, wc = e1[0:1, last:], e2[0:1, last:]
            c_s[h] = wc * c_prev + f * dc
            n_s[h:h + 1, :] = wc * n_prev + f * dn
            m_s[h:h + 1, :] = jnp.broadcast_to(b_row[:, last:] + jnp.maximum(m_prev, g_last), (1, LANES))

            hcen = h_t - (_all_sublanes(jnp.sum(h_t, axis=0), jnp.add) * (1.0 / M_HD))[None]
            var = _all_sublanes(jnp.sum(hcen * hcen, axis=0), jnp.add) * (1.0 / M_HD)
            hn = (hcen * lax.rsqrt(var + LN_EPS)[None]).reshape(M_HD, chunk).T
            o_ref[0, rows, hc] = (hn * nw_ref[:, hc] * jax.nn.sigmoid(mo_ref[0, rows, hc])).astype(BF16)

    @pl.when(t == n_t - 1)
    def _():
        cf_ref[0] = c_s[...]
        nf_ref[0] = n_s[...]
        mf_ref[0] = m_s[...]
        cvf_ref[0] = ubuf[0:SUBLANES, :]


def _mlstm(mqk, mv, mo, gc, gr, conv_w, conv_b, b_if, norm_w, c0, n0, m0, cv0, *, chunk, tb):
    b, t, width = mqk.shape
    tok = lambda n: pl.BlockSpec((1, tb, n), lambda i, j: (i, j, 0))
    per_b = lambda shape: pl.BlockSpec((1,) + shape, lambda i, j: (i,) + (0,) * len(shape))
    bc = jnp.zeros((1, LANES), F32).at[0, :N_GATES].set(b_if)
    br = b_if.reshape(N_GATES, 1)
    out_shape = (
        jax.ShapeDtypeStruct((b, t, M_WIDTH), BF16),
        jax.ShapeDtypeStruct((b, M_HEADS, M_HD, M_HD), F32),
        jax.ShapeDtypeStruct((b, SUBLANES, M_HD), F32),
        jax.ShapeDtypeStruct((b, SUBLANES, LANES), F32),
        jax.ShapeDtypeStruct((b, SUBLANES, width), F32),
    )
    return pl.pallas_call(
        functools.partial(_mlstm_kernel, chunk=chunk, tb=tb),
        grid=(b, t // tb),
        in_specs=[
            tok(width), tok(M_WIDTH), tok(M_WIDTH), tok(LANES),
            pl.BlockSpec((1, tb // chunk, N_GATES, chunk), lambda i, j: (i, j, 0, 0)),
            _const_spec(conv_w.shape), _const_spec((1, width)), _const_spec(bc.shape), _const_spec(br.shape),
            _const_spec((1, M_WIDTH)),
            per_b((M_HEADS, M_HD, M_HD)), per_b((SUBLANES, M_HD)), per_b((SUBLANES, LANES)), per_b((SUBLANES, width)),
        ],
        out_specs=(tok(M_WIDTH), per_b((M_HEADS, M_HD, M_HD)), per_b((SUBLANES, M_HD)),
                   per_b((SUBLANES, LANES)), per_b((SUBLANES, width))),
        out_shape=out_shape,
        scratch_shapes=[
            pltpu.VMEM((SUBLANES + tb, width), F32),
            pltpu.VMEM((tb, width), BF16),
            pltpu.VMEM((M_HEADS, M_HD, M_HD), F32),
            pltpu.VMEM((SUBLANES, M_HD), F32),
            pltpu.VMEM((SUBLANES, LANES), F32),
        ],
        compiler_params=_params("parallel", "arbitrary"),
        name="mlstm",
    )(mqk, mv, mo, gc, gr, conv_w, conv_b.reshape(1, width), bc, br, norm_w.reshape(1, M_WIDTH), c0, n0, m0, cv0)


def _mix_kernel(x_ref, an_ref, mn_ref, sc_ref, sh_ref, g1_ref, wa_ref, wb_ref, wg_ref, bg_ref, wo_ref,
                lg_ref, lb_ref, o_ref, *, alpha):
    d = x_ref.shape[-1]
    for r in _row_parts(x_ref.shape[1]):
        x = x_ref[0, r, :]
        h = (_layer_norm(x) * (1.0 + _mod_rows(sc_ref, r)) + _mod_rows(sh_ref, r)).astype(BF16)
        y_a = _dot(an_ref[0, r, :], wa_ref[...])
        y_b = _dot(mn_ref[0, r, :], wb_ref[...])
        g_a = jax.nn.sigmoid(_dot(h, wg_ref[:, :d]) + bg_ref[:, :d])
        g_b = jax.nn.sigmoid(_dot(h, wg_ref[:, d:]) + bg_ref[:, d:])
        mix = _dot((g_a * y_a + g_b * y_b).astype(BF16), wo_ref[...])
        o_ref[0, r, :] = _layer_norm(alpha * x + (1.0 + _mod_rows(g1_ref, r)) * mix) * lg_ref[...] + lb_ref[...]


def _mix(x, an, mn, sc, sh, g1, w_a, w_b, w_g, b_g, w_o, ln_g, ln_b, *, tm, alpha):
    b, t, d = x.shape
    tok = lambda n: pl.BlockSpec((1, tm, n), lambda i, j: (i, j, 0))
    row = _mod_spec(sc, tm)
    return pl.pallas_call(
        functools.partial(_mix_kernel, alpha=alpha),
        grid=(b, t // tm),
        in_specs=[tok(d), tok(DA_WIDTH), tok(M_WIDTH), row, row, row,
                  _const_spec(w_a.shape), _const_spec(w_b.shape), _const_spec(w_g.shape), _const_spec((1, 2 * d)),
                  _const_spec(w_o.shape), _const_spec((1, d)), _const_spec((1, d))],
        out_specs=tok(d),
        out_shape=jax.ShapeDtypeStruct((b, t, d), F32),
        compiler_params=_params("parallel", "parallel"),
        name="mix_out",
    )(x, an, mn, sc, sh, g1, w_a, w_b, w_g, b_g.reshape(1, 2 * d), w_o, ln_g.reshape(1, d), ln_b.reshape(1, d))


def _ffn_kernel(x_ref, sc_ref, sh_ref, g2_ref, wgu_ref, wd_ref, lg_ref, lb_ref, o_ref, *, alpha, d_ff, fc):
    rows = x_ref.shape[1]
    n_parts = max(rows // FFN_PART_ROWS, 1)
    for p in range(n_parts):
        r = slice(p * rows // n_parts, (p + 1) * rows // n_parts)
        h = (_layer_norm(x_ref[0, r, :]) * (1.0 + _mod_rows(sc_ref, r)) + _mod_rows(sh_ref, r)).astype(BF16)
        for j in range(d_ff // fc):
            gt = _dot(h, wgu_ref[:, j * fc:(j + 1) * fc])
            up = _dot(h, wgu_ref[:, d_ff + j * fc:d_ff + (j + 1) * fc])
            part = _dot((jax.nn.silu(gt) * up).astype(BF16), wd_ref[j * fc:(j + 1) * fc, :])
            if j == 0:
                o_ref[0, r, :] = part
            else:
                o_ref[0, r, :] += part
        y = alpha * x_ref[0, r, :] + (1.0 + _mod_rows(g2_ref, r)) * o_ref[0, r, :]
        o_ref[0, r, :] = _layer_norm(y) * lg_ref[...] + lb_ref[...]


def _ffn(x, sc, sh, g2, w_gu, w_down, ln_g, ln_b, *, tm, alpha):
    b, t, d = x.shape
    d_ff = w_down.shape[0]
    fc = 256
    tok = pl.BlockSpec((1, tm, d), lambda i, j: (i, j, 0))
    row = _mod_spec(sc, tm)
    return pl.pallas_call(
        functools.partial(_ffn_kernel, alpha=alpha, d_ff=d_ff, fc=fc),
        grid=(b, t // tm),
        in_specs=[tok, row, row, row, _const_spec(w_gu.shape), _const_spec(w_down.shape),
                  _const_spec((1, d)), _const_spec((1, d))],
        out_specs=tok,
        out_shape=jax.ShapeDtypeStruct((b, t, d), F32),
        compiler_params=_params("parallel", "parallel"),
        name="swiglu",
    )(x, sc, sh, g2, w_gu, w_down, ln_g.reshape(1, d), ln_b.reshape(1, d))


def _tile(t, pref):
    return pref if t % pref == 0 else t


def _layer(x, mod, layer, depth, attend, mstate, weights, kv_prev):
    (w_a, w_if, w_ift, b_if, conv_w, conv_b, lam_p, da_norm_w, m_norm_w, w_br_a, w_br_b, w_gate, b_gate, w_o,
     ln1_g, ln1_b, w_gu, w_down, ln2_g, ln2_b) = weights
    b, t, d = x.shape
    alpha = (2 * depth) ** 0.25
    lam_init = 0.8 - 0.6 * math.exp(-0.3 * layer)
    mods = [m.reshape(b, 1, d) for m in jnp.split(mod, 6, axis=-1)]
    chunk = min(CHUNK, t)
    tb = _tile(t, 512)
    flat = t % LANES != 0
    fb, ft = (1, b * t) if flat else (b, t)
    if flat:
        mods = [jnp.broadcast_to(m, (b, t, d)).reshape(fb, ft, d) for m in mods]
    sh1, sc1, g1, sh2, sc2, g2 = mods
    tm = _tile(ft, 512)
    tokens = lambda a: a.reshape(fb, ft, a.shape[-1])
    sequences = lambda a: a.reshape(b, t, a.shape[-1])

    q, k, v, mqk, mv, mo, gc, gr = _inproj(tokens(x), sc1, sh1, w_a, w_if, w_ift, kv_prev,
                                           layer=layer, depth=depth, tm=tm, chunk=chunk)
    k_seq, v_seq = (a.reshape(depth, b, DA_HEADS * t, DA_QK) for a in (k, v))
    a_n = attend(sequences(q), k_seq, v_seq, lam_p, da_norm_w.reshape(1, DA_WIDTH), layer, lam_init)
    m_n, c_f, n_f, m_f, cv_f = _mlstm(sequences(mqk), sequences(mv), sequences(mo), sequences(gc),
                                      gr.reshape(b, t // chunk, N_GATES, chunk), conv_w, conv_b, b_if, m_norm_w,
                                      *mstate, chunk=chunk, tb=tb)
    y = _mix(tokens(x), tokens(a_n), tokens(m_n), sc1, sh1, g1, w_br_a, w_br_b, w_gate, b_gate, w_o, ln1_g, ln1_b,
             tm=tm, alpha=alpha)
    y = _ffn(y, sc2, sh2, g2, w_gu, w_down, ln2_g, ln2_b, tm=_tile(ft, 2 * FFN_PART_ROWS), alpha=alpha)
    state = (c_f, n_f[:, :M_HEADS], m_f[:, :M_HEADS, 0], cv_f[:, SUBLANES - (CONV_W - 1):])
    return sequences(y), (k, v), state


def _pad_rows(a, rows):
    pad = [(0, 0)] * a.ndim
    pad[1] = (rows - a.shape[1], 0)
    return jnp.pad(a, pad)


def kernel(x_prompt, x_sample, c_prompt, c_sample, cache_attn_k, cache_attn_v, state_mlstm_C, state_mlstm_n,
           state_mlstm_m, state_mlstm_conv, w_ada, b_ada, w_in, b_if, conv_w, conv_b, lam_p, da_norm_w, m_norm_w,
           w_br_a, w_br_b, w_gate, b_gate, w_o, ln1_g, ln1_b, w_gu, w_down, ln2_g, ln2_b):
    depth = w_in.shape[0]
    bp, bs = x_prompt.shape[0], x_sample.shape[0]
    past = cache_attn_k.shape[2]

    mod = _ada(jnp.concatenate([c_prompt, c_sample], axis=0), w_ada.astype(BF16), b_ada)

    gate_lo = 3 * DA_WIDTH + 2 * M_WIDTH + M_WIDTH
    gate_hi = gate_lo + N_GATES

    cache_k = cache_attn_k.reshape(depth, bs, past * DA_HEADS, DA_QK)
    cache_v = cache_attn_v.reshape(depth, bs, past * DA_HEADS, DA_VD)

    xp, xs = x_prompt, x_sample
    kv_p = kv_s = None
    outs_p, outs_s = [], []
    for l in range(depth):
        w_l = w_in[l]
        w_a = jnp.concatenate([w_l[:, :gate_lo], w_l[:, gate_hi:]], axis=1).astype(BF16)
        w_if = jnp.pad(w_l[:, gate_lo:gate_hi], ((0, 0), (0, LANES - N_GATES))).astype(BF16)
        w_ift = w_l[:, gate_lo:gate_hi].T.astype(BF16)
        weights = (w_a, w_if, w_ift, b_if[l], conv_w[l], conv_b[l], lam_p[l], da_norm_w[l], m_norm_w[l],
                   w_br_a[l].astype(BF16), w_br_b[l].astype(BF16), w_gate[l].astype(BF16), b_gate[l],
                   w_o[l].astype(BF16), ln1_g[l], ln1_b[l], w_gu[l].astype(BF16), w_down[l].astype(BF16),
                   ln2_g[l], ln2_b[l])

        zero_state = (jnp.zeros((bp, M_HEADS, M_HD, M_HD), F32), jnp.zeros((bp, SUBLANES, M_HD), F32),
                      jnp.zeros((bp, SUBLANES, LANES), F32), jnp.zeros((bp, SUBLANES, 2 * M_WIDTH), F32))
        attend_p = lambda q, k, v, lp, g, layer, lam_init: _attn_prompt(
            q, k, v, lp, g, layer=layer, lam_init=lam_init, tq=_tile(q.shape[1], 256))
        xp, kv_p, st_p = _layer(xp, mod[l, :bp], l, depth, attend_p, zero_state, weights, kv_p)
        outs_p.append(st_p)

        attend_s = lambda q, k, v, lp, g, layer, lam_init: _attn_sample(
            q, k, v, cache_k, cache_v, lp, g, layer=layer, lam_init=lam_init)
        head_pad = ((0, 0), (0, SUBLANES - M_HEADS), (0, 0))
        state_s = (state_mlstm_C[l],
                   jnp.pad(state_mlstm_n[l], head_pad),
                   jnp.pad(jnp.broadcast_to(state_mlstm_m[l][:, :, None], (bs, M_HEADS, LANES)), head_pad),
                   _pad_rows(state_mlstm_conv[l], SUBLANES))
        xs, kv_s, st_s = _layer(xs, mod[l, bp:], l, depth, attend_s, state_s, weights, kv_s)
        outs_s.append(st_s)

    stack = lambda outs, i: jnp.stack([o[i] for o in outs])
    kp, vp = (a.reshape(depth, bp, x_prompt.shape[1], DA_HEADS, DA_QK) for a in kv_p)
    ks, vs = (a.reshape(depth, bs, x_sample.shape[1], DA_HEADS, DA_QK) for a in kv_s)
    cp, np_, mp, cvp = (stack(outs_p, i) for i in range(4))
    cs, ns, ms, cvs = (stack(outs_s, i) for i in range(4))
    return (xp, xs, kp, vp, ks, vs, cp, np_, mp, cvp, cs, ns, ms, cvs)
```
